```python
import jax, jax.numpy as jnp
from jax import lax
import numpy as np

D_MODEL = 4096
BATCH = 2
SEQ = 4096
DEPTH = 2

N_MIXERS = 2
N_POOL_LAYERS = (DEPTH + 1) // 2
N_RWKV_LAYERS = DEPTH // 2
POOL_WINDOWS = (2, 4, 8, 16)
N_POOL_GROUPS = len(POOL_WINDOWS)
POOL_GROUP = D_MODEL // N_POOL_GROUPS
RWKV_HEAD = 64
RWKV_HEADS = D_MODEL // RWKV_HEAD


def _lora_dim(factor, power):
    return max(32, int(round(factor * D_MODEL ** power / 32)) * 32)


DECAY_LORA = _lora_dim(1.8, 0.5)
AAA_LORA = _lora_dim(1.8, 0.5)
GATE_LORA = _lora_dim(0.6, 0.8)
N_SHIFT_MIX = 6
LNX_EPS = 64e-5
D_FF = ((8 * D_MODEL // 3 + 255) // 256) * 256
FFN_CONV = 3
N_ADA = 6
NORM_EPS = 1e-6

kernel_name = "hybrid_pool_rwkv7_convglu_adaln"


def rmsnorm(x, g):
    xf = x.astype(jnp.float32)
    xf = xf * lax.rsqrt(jnp.mean(xf * xf, axis=-1, keepdims=True) + NORM_EPS)
    return (xf * g.astype(jnp.float32)).astype(x.dtype)


def pool_mixer(h, w_grp, scale):
    B, T, D = h.shape
    cs = jnp.cumsum(h.astype(jnp.float32), axis=1)
    pos = jnp.arange(1, T + 1, dtype=jnp.float32)
    outs = []
    for gi, win in enumerate(POOL_WINDOWS):
        csg = cs[..., gi * POOL_GROUP:(gi + 1) * POOL_GROUP]
        lag = jnp.pad(csg, ((0, 0), (win, 0), (0, 0)))[:, :T]
        cnt = jnp.minimum(pos, float(win))[None, :, None]
        outs.append((csg - lag) / cnt)
    pooled = jnp.concatenate(outs, axis=-1).astype(h.dtype) - h
    y = jnp.einsum('btgc,gce->btge', pooled.reshape(B, T, N_POOL_GROUPS, POOL_GROUP), w_grp)
    return y.reshape(B, T, D) * scale


def wkv7_scan(r, decay, k, v, a, b):
    B, T, H, N = r.shape

    def step(S, inp):
        r_t, w_t, k_t, v_t, a_t, b_t = inp
        sa = jnp.einsum('bhvk,bhk->bhv', S, a_t)
        S = (S * w_t[:, :, None, :] + sa[..., None] * b_t[:, :, None, :]
             + v_t[..., None] * k_t[:, :, None, :])
        return S, jnp.einsum('bhvk,bhk->bhv', S, r_t)

    seq = tuple(jnp.moveaxis(z, 1, 0) for z in (r, decay, k, v, a, b))
    S0 = jnp.zeros((B, H, N, N), jnp.float32)
    _, y = lax.scan(step, S0, seq)
    return jnp.moveaxis(y, 0, 1)


def rwkv7_time_mix(h, mu, wr, wk, wv, w0, w1, w2, a0, a1, a2, g1, g2,
                   k_k, k_a, r_k, lnx_g, lnx_b, wo):
    B, T, D = h.shape
    H, N = RWKV_HEADS, RWKV_HEAD
    f32 = jnp.float32
    dx = jnp.pad(h, ((0, 0), (1, 0), (0, 0)))[:, :T] - h
    xr, xw, xk, xv, xa, xg = [h + dx * mu[n] for n in range(N_SHIFT_MIX)]
    r = xr @ wr
    k = xk @ wk
    v = xv @ wv
    w = -jax.nn.softplus(-(w0 + jnp.tanh(xw @ w1) @ w2).astype(f32)) - 0.5
    a = jax.nn.sigmoid((a0 + (xa @ a1) @ a2).astype(f32))
    g = jax.nn.sigmoid(xg @ g1) @ g2
    kk = (k * k_k).astype(f32).reshape(B, T, H, N)
    kk = kk / jnp.maximum(jnp.linalg.norm(kk, axis=-1, keepdims=True), 1e-12)
    k = k.astype(f32) * (1.0 + (a - 1.0) * k_a.astype(f32))
    rh = r.astype(f32).reshape(B, T, H, N)
    kh = k.reshape(B, T, H, N)
    vh = v.astype(f32).reshape(B, T, H, N)
    ah = a.reshape(B, T, H, N)
    decay = jnp.exp(-jnp.exp(w)).reshape(B, T, H, N)
    y = wkv7_scan(rh, decay, kh, vh, -kk, kk * ah)
    mean = jnp.mean(y, axis=-1, keepdims=True)
    var = jnp.mean(jnp.square(y - mean), axis=-1, keepdims=True)
    y = ((y - mean) * lax.rsqrt(var + LNX_EPS)).reshape(B, T, D)
    y = y * lnx_g.astype(f32) + lnx_b.astype(f32)
    bonus = jnp.sum(rh * kh * r_k.astype(f32), axis=-1, keepdims=True) * vh
    y = (y + bonus.reshape(B, T, D)).astype(h.dtype)
    return (y * g) @ wo


def conv_glu(h, w_up, conv_w, conv_b, w_down):
    gate, val = jnp.split(h @ w_up, 2, axis=-1)
    gate = lax.conv_general_dilated(
        gate, conv_w[:, None, :], window_strides=(1,),
        padding=[(FFN_CONV - 1, 0)], dimension_numbers=('NWC', 'WIO', 'NWC'),
        feature_group_count=gate.shape[-1]) + conv_b
    return (jax.nn.gelu(gate, approximate=False) * val) @ w_down


def setup_inputs(seed: int = 0) -> dict:
    key = jax.random.key(seed)
    ks = iter(jax.random.split(key, 32))
    D, F, H, N = D_MODEL, D_FF, RWKV_HEADS, RWKV_HEAD
    LP, LR = N_POOL_LAYERS, N_RWKV_LAYERS
    f32 = jnp.float32

    def nrm(shape, fan_in, s=1.0):
        return jax.random.normal(next(ks), shape, f32) * (s * fan_in ** -0.5)

    def gain(shape, s=0.05):
        return 1.0 + s * jax.random.normal(next(ks), shape, f32)

    def small(shape, s=0.02):
        return s * jax.random.normal(next(ks), shape, f32)

    def unif(shape, lo, hi):
        return jax.random.uniform(next(ks), shape, f32, lo, hi)

    return {
        "x": jax.random.normal(next(ks), (BATCH, SEQ, D), f32),
        "c": jax.random.normal(next(ks), (BATCH, D), f32),
        "ada_w": nrm((DEPTH, D, N_ADA * D), D, 0.5),
        "ada_b": small((DEPTH, N_ADA * D)),
        "norm_g": gain((DEPTH, 2, D)),
        "pool_w": nrm((LP, N_POOL_GROUPS, POOL_GROUP, POOL_GROUP), POOL_GROUP),
        "pool_scale": gain((LP, D), 0.1),
        "rwkv_mu": unif((LR, N_SHIFT_MIX, D), 0.0, 1.0),
        "rwkv_wr": nrm((LR, D, D), D),
        "rwkv_wk": nrm((LR, D, D), D),
        "rwkv_wv": nrm((LR, D, D), D),
        "rwkv_w0": unif((LR, D), -6.5, -1.5),
        "rwkv_w1": nrm((LR, D, DECAY_LORA), D),
        "rwkv_w2": nrm((LR, DECAY_LORA, D), DECAY_LORA, 0.1),
        "rwkv_a0": small((LR, D), 0.1),
        "rwkv_a1": nrm((LR, D, AAA_LORA), D),
        "rwkv_a2": nrm((LR, AAA_LORA, D), AAA_LORA, 0.1),
        "rwkv_g1": nrm((LR, D, GATE_LORA), D),
        "rwkv_g2": nrm((LR, GATE_LORA, D), GATE_LORA),
        "rwkv_kk": 0.85 + 0.05 * jax.random.normal(next(ks), (LR, D), f32),
        "rwkv_ka": gain((LR, D)),
        "rwkv_rk": small((LR, H, N), 0.1),
        "rwkv_lnx_g": gain((LR, D)),
        "rwkv_lnx_b": small((LR, D)),
        "rwkv_wo": nrm((LR, D, D), D),
        "ffn_w_up": nrm((DEPTH, D, 2 * F), D),
        "ffn_conv_w": nrm((DEPTH, FFN_CONV, F), FFN_CONV),
        "ffn_conv_b": small((DEPTH, F)),
        "ffn_w_down": nrm((DEPTH, F, D), F),
        "final_g": gain((D,)),
    }


def reference(x, c, ada_w, ada_b, norm_g, pool_w, pool_scale, rwkv_mu, rwkv_wr,
              rwkv_wk, rwkv_wv, rwkv_w0, rwkv_w1, rwkv_w2, rwkv_a0, rwkv_a1,
              rwkv_a2, rwkv_g1, rwkv_g2, rwkv_kk, rwkv_ka, rwkv_rk, rwkv_lnx_g,
              rwkv_lnx_b, rwkv_wo, ffn_w_up, ffn_conv_w, ffn_conv_b, ffn_w_down,
              final_g):
    c_act = jax.nn.silu(c)
    for i in range(DEPTH):
        mod = c_act @ ada_w[i] + ada_b[i]
        sh1, sc1, gt1, sh2, sc2, gt2 = [m[:, None, :] for m in jnp.split(mod, N_ADA, axis=-1)]
        h = rmsnorm(x, norm_g[i, 0]) * (1.0 + sc1) + sh1
        j = i // N_MIXERS
        if i % N_MIXERS == 0:
            y = pool_mixer(h, pool_w[j], pool_scale[j])
        else:
            y = rwkv7_time_mix(h, rwkv_mu[j], rwkv_wr[j], rwkv_wk[j], rwkv_wv[j],
                               rwkv_w0[j], rwkv_w1[j], rwkv_w2[j], rwkv_a0[j],
                               rwkv_a1[j], rwkv_a2[j], rwkv_g1[j], rwkv_g2[j],
                               rwkv_kk[j], rwkv_ka[j], rwkv_rk[j], rwkv_lnx_g[j],
                               rwkv_lnx_b[j], rwkv_wo[j])
        x = x + gt1 * y
        h = rmsnorm(x, norm_g[i, 1]) * (1.0 + sc2) + sh2
        x = x + gt2 * conv_glu(h, ffn_w_up[i], ffn_conv_w[i], ffn_conv_b[i], ffn_w_down[i])
    return rmsnorm(x, final_g)
```

```python
import functools

import jax
import jax.numpy as jnp
from jax import lax
from jax.experimental import pallas as pl
from jax.experimental.pallas import tpu as pltpu

F32 = jnp.float32
BF16 = jnp.bfloat16

POOL_WINDOWS = (2, 4, 8, 16)
POOL_HALO = 16
HEAD = 64
PAIR = 2 * HEAD
CHUNK = 64
N_SHIFT_MIX = 6
N_ADA = 6
NORM_EPS = 1e-6
LNX_EPS = 64e-5
SUBLANES = 8
VMEM_LIMIT = 56 * 1024 * 1024
ROWS_POOL = 256
ROWS_NORM = 256
ROWS_MIX = 128
ROWS_FFN = 512
COLS_FFN = 256
COLS_FFN_OUT = 512
ROWS_LORA = 256
TILE_DENSE = 1024
COLS_ADA = 512
LANES_WKV = 4 * PAIR


def _params(*sem):
    return pltpu.CompilerParams(dimension_semantics=sem, vmem_limit_bytes=VMEM_LIMIT)


def _tile(n, want):
    t = min(n, want)
    assert n % t == 0, (n, want)
    return t


def _ada_kernel(c_ref, w_ref, b_ref, o_ref):
    c = c_ref[...]
    ca = (c * jax.nn.sigmoid(c)).astype(BF16)
    o_ref[...] = jnp.dot(ca, w_ref[...].astype(BF16), preferred_element_type=F32) + b_ref[...]


def _ada(c, ada_w, ada_b):
    depth, d, n = ada_w.shape
    b = c.shape[0]
    cp = jnp.zeros((SUBLANES, d), F32).at[:b].set(c)
    tn = _tile(n, COLS_ADA)
    out = pl.pallas_call(
        _ada_kernel,
        grid=(depth, n // tn),
        in_specs=[
            pl.BlockSpec((SUBLANES, d), lambda l, j: (0, 0)),
            pl.BlockSpec((None, d, tn), lambda l, j: (l, 0, j)),
            pl.BlockSpec((None, 1, tn), lambda l, j: (l, 0, j)),
        ],
        out_specs=pl.BlockSpec((None, SUBLANES, tn), lambda l, j: (l, 0, j)),
        out_shape=jax.ShapeDtypeStruct((depth, SUBLANES, n), F32),
        compiler_params=_params("arbitrary", "arbitrary"),
        name="ada_mod",
    )(cp, ada_w, ada_b.reshape(depth, 1, n))
    return out[:, :b]


def _rms_mod(x, g, sc, sh):
    ms = jnp.mean(x * x, axis=-1, keepdims=True)
    return (x * lax.rsqrt(ms + NORM_EPS)) * g * (1.0 + sc) + sh


def _row_spec(d):
    return pl.BlockSpec((None, 1, d), lambda b, t: (b, 0, 0))


def _par_spec(d):
    return pl.BlockSpec((1, d), lambda b, t: (0, 0))


def _pool_kernel(x_ref, g1_ref, sc1_ref, sh1_ref, gt1_ref, pw_ref, ps_ref,
                 g2_ref, sc2_ref, sh2_ref, x1_ref, h2_ref, carry_ref, *, tt, group):
    t = pl.program_id(1)

    @pl.when(t == 0)
    def _():
        carry_ref[...] = jnp.zeros_like(carry_ref)

    x = x_ref[...]
    h = _rms_mod(x, g1_ref[...], sc1_ref[...], sh1_ref[...])
    pos = (t * tt + lax.broadcasted_iota(jnp.int32, (tt, 1), 0) + 1).astype(F32)
    ys = []
    for gi, win in enumerate(POOL_WINDOWS):
        sl = slice(gi * group, (gi + 1) * group)
        hg = h[:, sl]
        s = jnp.concatenate([carry_ref[:, sl], hg], axis=0)
        shift = 1
        while shift < win:
            s = s + pltpu.roll(s, shift, axis=0)
            shift *= 2
        pooled = s[POOL_HALO:, :] / jnp.minimum(pos, float(win)) - hg
        ys.append(jnp.dot(pooled.astype(BF16), pw_ref[gi], preferred_element_type=F32))
    carry_ref[...] = h[tt - POOL_HALO:, :]
    y = jnp.concatenate(ys, axis=1) * ps_ref[...]
    x1 = x + gt1_ref[...] * y
    x1_ref[...] = x1
    h2_ref[...] = _rms_mod(x1, g2_ref[...], sc2_ref[...], sh2_ref[...]).astype(BF16)


def _pool_layer(x, g1, sc1, sh1, gt1, pool_w, pool_scale, g2, sc2, sh2):
    b, t, d = x.shape
    group = d // len(POOL_WINDOWS)
    tt = _tile(t, ROWS_POOL)
    tile = pl.BlockSpec((None, tt, d), lambda bi, ti: (bi, ti, 0))
    return pl.pallas_call(
        functools.partial(_pool_kernel, tt=tt, group=group),
        grid=(b, t // tt),
        in_specs=[tile, _par_spec(d), _row_spec(d), _row_spec(d), _row_spec(d),
                  pl.BlockSpec(pool_w.shape, lambda bi, ti: (0, 0, 0)), _par_spec(d),
                  _par_spec(d), _row_spec(d), _row_spec(d)],
        out_specs=[tile, tile],
        out_shape=[jax.ShapeDtypeStruct((b, t, d), F32), jax.ShapeDtypeStruct((b, t, d), BF16)],
        scratch_shapes=[pltpu.VMEM((POOL_HALO, d), F32)],
        compiler_params=_params("arbitrary", "arbitrary"),
        name="pool_mixer",
    )(x, g1, sc1, sh1, gt1, pool_w, pool_scale, g2, sc2, sh2)


def _gelu(x):
    return 0.5 * x * (1.0 + lax.erf(x * (2.0 ** -0.5)))


def _ffn_kernel(h_ref, wg_ref, wv_ref, cw_ref, cb_ref, wd_ref, y_ref, carry_ref,
                *, tm, tn, tiles_per_seq):
    i = pl.program_id(0)
    j = pl.program_id(1)
    h = h_ref[...]
    ug = jnp.dot(h, wg_ref[...], preferred_element_type=F32)
    uv = jnp.dot(h, wv_ref[...], preferred_element_type=F32)
    prev = jnp.where(i % tiles_per_seq == 0, 0.0, carry_ref[j])
    carry_ref[j] = ug[tm - SUBLANES:, :]
    ext = jnp.concatenate([prev, ug], axis=0)
    u1 = ext[SUBLANES - 1:SUBLANES - 1 + tm, :]
    u2 = ext[SUBLANES - 2:SUBLANES - 2 + tm, :]
    cw = cw_ref[...]
    gate = ug * cw[2:3, :] + u1 * cw[1:2, :] + u2 * cw[0:1, :] + cb_ref[...]
    act = (_gelu(gate) * uv).astype(BF16)
    d = y_ref.shape[-1]
    for n in range(d // tn):
        cols = slice(n * tn, (n + 1) * tn)
        part = jnp.dot(act, wd_ref[:, cols], preferred_element_type=F32)

        @pl.when(j == 0)
        def _():
            y_ref[:, cols] = part

        @pl.when(j > 0)
        def _():
            y_ref[:, cols] += part


def _ffn(h, w_up, conv_w, conv_b, w_down, seq_len):
    m, d = h.shape
    f = w_down.shape[0]
    tm = _tile(seq_len, ROWS_FFN)
    tf = _tile(f, COLS_FFN)
    nf = f // tf
    return pl.pallas_call(
        functools.partial(_ffn_kernel, tm=tm, tn=_tile(d, COLS_FFN_OUT), tiles_per_seq=seq_len // tm),
        grid=(m // tm, nf),
        in_specs=[
            pl.BlockSpec((tm, d), lambda i, j: (i, 0)),
            pl.BlockSpec((d, tf), lambda i, j: (0, j)),
            pl.BlockSpec((d, tf), lambda i, j: (0, j + nf)),
            pl.BlockSpec((conv_w.shape[0], tf), lambda i, j: (0, j)),
            pl.BlockSpec((1, tf), lambda i, j: (0, j)),
            pl.BlockSpec((tf, d), lambda i, j: (j, 0)),
        ],
        out_specs=pl.BlockSpec((tm, d), lambda i, j: (i, 0)),
        out_shape=jax.ShapeDtypeStruct((m, d), F32),
        scratch_shapes=[pltpu.VMEM((nf, SUBLANES, tf), F32)],
        compiler_params=_params("arbitrary", "arbitrary"),
        name="conv_glu",
    )(h, w_up, w_up, conv_w, conv_b.reshape(1, f), w_down)


def _resnorm_kernel(x_ref, y_ref, gt_ref, g_ref, sc_ref, sh_ref, xo_ref, h_ref):
    x = x_ref[...] + gt_ref[...] * y_ref[...]
    xo_ref[...] = x
    h_ref[...] = _rms_mod(x, g_ref[...], sc_ref[...], sh_ref[...]).astype(h_ref.dtype)


def _resnorm(x, y, gt, g, sc, sh):
    b, t, d = x.shape
    tt = _tile(t, ROWS_NORM)
    tile = pl.BlockSpec((None, tt, d), lambda bi, ti: (bi, ti, 0))
    return pl.pallas_call(
        _resnorm_kernel,
        grid=(b, t // tt),
        in_specs=[tile, tile, _row_spec(d), _par_spec(d), _row_spec(d), _row_spec(d)],
        out_specs=[tile, tile],
        out_shape=[jax.ShapeDtypeStruct((b, t, d), F32), jax.ShapeDtypeStruct((b, t, d), BF16)],
        compiler_params=_params("arbitrary", "arbitrary"),
        name="residual_norm",
    )(x, y, gt, g, sc, sh)


def _resnorm_mix_kernel(x_ref, y_ref, gt_ref, g_ref, sc_ref, sh_ref, mu_ref, xo_ref, mix_ref,
                        carry_ref, *, tt):
    t = pl.program_id(1)

    @pl.when(t == 0)
    def _():
        carry_ref[...] = jnp.zeros_like(carry_ref)

    x = x_ref[...] + gt_ref[...] * y_ref[...]
    xo_ref[...] = x
    h = _rms_mod(x, g_ref[...], sc_ref[...], sh_ref[...])
    first_row = lax.broadcasted_iota(jnp.int32, (tt, 1), 0) == 0
    prev = jnp.where(first_row, carry_ref[SUBLANES - 1:SUBLANES, :], pltpu.roll(h, 1, axis=0))
    carry_ref[...] = h[tt - SUBLANES:, :]
    dx = prev - h
    mu = mu_ref[...]
    for n in range(N_SHIFT_MIX):
        mix_ref[n] = (h + dx * mu[n:n + 1, :]).astype(BF16)


def _resnorm_mix(x, y, gt, g, sc, sh, mu):
    b, t, d = x.shape
    tt = _tile(t, ROWS_MIX)
    tile = pl.BlockSpec((None, tt, d), lambda bi, ti: (bi, ti, 0))
    return pl.pallas_call(
        functools.partial(_resnorm_mix_kernel, tt=tt),
        grid=(b, t // tt),
        in_specs=[tile, tile, _row_spec(d), _par_spec(d), _row_spec(d), _row_spec(d),
                  pl.BlockSpec(mu.shape, lambda bi, ti: (0, 0))],
        out_specs=[tile, pl.BlockSpec((N_SHIFT_MIX, None, tt, d), lambda bi, ti: (0, bi, ti, 0))],
        out_shape=[jax.ShapeDtypeStruct((b, t, d), F32),
                   jax.ShapeDtypeStruct((N_SHIFT_MIX, b, t, d), BF16)],
        scratch_shapes=[pltpu.VMEM((SUBLANES, d), F32)],
        compiler_params=_params("arbitrary", "arbitrary"),
        name="residual_norm_mix",
    )(x, y, gt, g, sc, sh, mu)


def _final_kernel(x_ref, y_ref, gt_ref, g_ref, o_ref):
    x = x_ref[...] + gt_ref[...] * y_ref[...]
    ms = jnp.mean(x * x, axis=-1, keepdims=True)
    o_ref[...] = (x * lax.rsqrt(ms + NORM_EPS)) * g_ref[...]


def _final(x, y, gt, g):
    b, t, d = x.shape
    tt = _tile(t, ROWS_NORM)
    tile = pl.BlockSpec((None, tt, d), lambda bi, ti: (bi, ti, 0))
    return pl.pallas_call(
        _final_kernel,
        grid=(b, t // tt),
        in_specs=[tile, tile, _row_spec(d), _par_spec(d)],
        out_specs=tile,
        out_shape=jax.ShapeDtypeStruct((b, t, d), F32),
        compiler_params=_params("arbitrary", "arbitrary"),
        name="final_norm",
    )(x, y, gt, g)


def _matmul_kernel(a_ref, w_ref, o_ref):
    o_ref[...] = jnp.dot(a_ref[...], w_ref[...], preferred_element_type=F32)


def _matmul(a, w):
    _, m, k = a.shape
    g, _, n = w.shape
    tm = _tile(m, TILE_DENSE)
    tn = _tile(n, TILE_DENSE)
    return pl.pallas_call(
        _matmul_kernel,
        grid=(g, m // tm, n // tn),
        in_specs=[pl.BlockSpec((None, tm, k), lambda gi, i, j: (gi, i, 0)),
                  pl.BlockSpec((None, k, tn), lambda gi, i, j: (gi, 0, j))],
        out_specs=pl.BlockSpec((None, tm, tn), lambda gi, i, j: (gi, i, j)),
        out_shape=jax.ShapeDtypeStruct((g, m, n), F32),
        compiler_params=_params("arbitrary", "arbitrary", "arbitrary"),
        name="dense",
    )(a, w)


def _lora_kernel(x_ref, w1_ref, w2_ref, o_ref, *, act):
    t = jnp.dot(x_ref[...], w1_ref[...], preferred_element_type=F32)
    if act == "tanh":
        t = jnp.tanh(t)
    elif act == "sigmoid":
        t = jax.nn.sigmoid(t)
    o_ref[...] = jnp.dot(t.astype(BF16), w2_ref[...], preferred_element_type=F32)


def _lora(mix, which, w1, w2, act):
    _, m, d = mix.shape
    r = w1.shape[1]
    tm = _tile(m, ROWS_LORA)
    return pl.pallas_call(
        functools.partial(_lora_kernel, act=act),
        grid=(m // tm,),
        in_specs=[pl.BlockSpec((None, tm, d), lambda i: (which, i, 0)),
                  pl.BlockSpec((d, r), lambda i: (0, 0)),
                  pl.BlockSpec((r, d), lambda i: (0, 0))],
        out_specs=pl.BlockSpec((tm, d), lambda i: (i, 0)),
        out_shape=jax.ShapeDtypeStruct((m, d), F32),
        compiler_params=_params("arbitrary"),
        name="lora_" + act,
    )(mix, w1, w2)


def _split(x):
    hi = x.astype(BF16)
    lo = (x - hi.astype(F32)).astype(BF16)
    return hi, lo


def _dg(a, b, ca, cb):
    return lax.dot_general(a, b, (((ca,), (cb,)), ((), ())), preferred_element_type=F32)


def _mm3(a, b, ca, cb):
    a_hi, a_lo = _split(a)
    b_hi, b_lo = _split(b)
    return _dg(jnp.concatenate([a_hi, a_hi, a_lo], axis=ca),
               jnp.concatenate([b_hi, b_lo, b_hi], axis=cb), ca, cb)


def _mm1(a, b, ca, cb):
    return _dg(a.astype(BF16), b.astype(BF16), ca, cb)


def _wkv_pair(r, k, v, wl, al, g, prm, s0, c):
    L = r.shape[0]
    w0, a0, kkp, kap, rkp, lng, lnb = [prm[i:i + 1, :] for i in range(7)]

    def stack(x):
        return jnp.concatenate([jnp.where(c["head0"], x, 0.0), jnp.where(c["head0"], 0.0, x)], axis=0)

    def headsum(x):
        hi, lo = _split(x)
        return _dg(jnp.concatenate([hi, lo], axis=1), c["ones2"], 1, 0)

    z = -(w0 + wl)
    w = -(jnp.maximum(z, 0.0) + jnp.log1p(jnp.exp(-jnp.abs(z)))) - 0.5
    logd = -jnp.exp(w)
    lr = jax.nn.sigmoid(a0 + al)
    kkr = k * kkp
    k2 = k * (1.0 + (lr - 1.0) * kap)
    hs = headsum(jnp.concatenate([kkr * kkr, r * k2 * rkp], axis=0))
    kk = kkr / jnp.maximum(jnp.sqrt(hs[:L]), 1e-12)
    bonus = hs[L:]
    ld_hi, ld_lo = _split(logd)
    cum = _dg(c["tril2"], jnp.concatenate([ld_hi, ld_lo], axis=0), 1, 0)
    p_t = jnp.exp(cum)
    inv_p = jnp.exp(-cum)
    rt = r * p_t
    at = -kk * jnp.exp(cum - logd)
    bt = kk * lr * inv_p
    kt = k2 * inv_p

    bks = jnp.concatenate([stack(bt), stack(kt)], axis=0)
    aa = _mm3(at, bks, 1, 1)
    n = jnp.where(c["strict"], aa[:, :PAIR], 0.0)
    aak = jnp.where(c["strict"], aa[:, PAIR:], 0.0)
    mm = _mm1(rt, bks, 1, 1)
    mrb = jnp.where(c["incl"], mm[:, :PAIR], 0.0)
    mrk = jnp.where(c["incl"], mm[:, PAIR:], 0.0)

    tinv = c["eye"] + n
    npow = n
    span = 2
    while span < L:
        nps = stack(npow)
        npow = _mm3(npow, nps, 1, 0)
        tinv = tinv + _mm3(tinv, stack(npow), 1, 0)
        span *= 2

    akv = _mm3(aak, stack(v), 1, 0)
    tu = _mm3(tinv, jnp.concatenate([stack(at), stack(akv)], axis=1), 1, 0)
    u = _mm3(tu[:, :PAIR], s0, 1, 1) + tu[:, PAIR:]
    y = _mm1(rt, s0, 1, 1) + _mm1(jnp.concatenate([mrb, mrk], axis=1),
                                  jnp.concatenate([stack(u), stack(v)], axis=0), 1, 0)
    p_last = p_t[L - 1:L, :]
    s_new = s0 * p_last + _mm3(jnp.concatenate([u, v], axis=0),
                               jnp.concatenate([bt, kt], axis=0) * p_last, 0, 0)
    s_new = jnp.where(c["blockdiag"], s_new, 0.0)

    mean = headsum(y) * (1.0 / HEAD)
    dlt = y - mean
    var = headsum(dlt * dlt) * (1.0 / HEAD)
    yn = dlt * lax.rsqrt(var + LNX_EPS) * lng + lnb
    return (yn + bonus * v) * g, s_new


def _wkv_consts(L):
    lane = lax.broadcasted_iota(jnp.int32, (L, PAIR), 1)
    row = lax.broadcasted_iota(jnp.int32, (L, PAIR), 0)
    src = lane % HEAD
    r2 = lax.broadcasted_iota(jnp.int32, (PAIR, PAIR), 0) // HEAD
    c2 = lax.broadcasted_iota(jnp.int32, (PAIR, PAIR), 1) // HEAD
    blockdiag = r2 == c2
    ones = jnp.where(blockdiag, 1.0, 0.0).astype(BF16)
    tr = lax.broadcasted_iota(jnp.int32, (L, L), 0)
    tc = lax.broadcasted_iota(jnp.int32, (L, L), 1)
    tril = jnp.where(tr >= tc, 1.0, 0.0).astype(BF16)
    return {
        "head0": lane < HEAD,
        "strict": row > src,
        "incl": row >= src,
        "eye": jnp.where(row == src, 1.0, 0.0),
        "blockdiag": blockdiag,
        "ones2": jnp.concatenate([ones, ones], axis=0),
        "tril2": jnp.concatenate([tril, tril], axis=1),
    }


def _wkv_kernel(r_ref, k_ref, v_ref, wl_ref, al_ref, g_ref, prm_ref, o_ref, s_ref, *, pairs):
    @pl.when(pl.program_id(2) == 0)
    def _():
        s_ref[...] = jnp.zeros_like(s_ref)

    c = _wkv_consts(r_ref.shape[0])
    for p in range(pairs):
        cols = slice(p * PAIR, (p + 1) * PAIR)
        out, s_new = _wkv_pair(r_ref[:, cols], k_ref[:, cols], v_ref[:, cols], wl_ref[:, cols],
                               al_ref[:, cols], g_ref[:, cols], prm_ref[:, cols], s_ref[p], c)
        s_ref[p] = s_new
        o_ref[:, cols] = out.astype(o_ref.dtype)


def _wkv(rkv, wl, al, g, prm, batch, seq_len):
    _, m, d = rkv.shape
    L = _tile(seq_len, CHUNK)
    lanes = _tile(d, LANES_WKV)
    nchunk = seq_len // L

    def tok(which):
        if which is None:
            return pl.BlockSpec((L, lanes), lambda b, hg, ci: (b * nchunk + ci, hg))
        return pl.BlockSpec((None, L, lanes), lambda b, hg, ci: (which, b * nchunk + ci, hg))

    return pl.pallas_call(
        functools.partial(_wkv_kernel, pairs=lanes // PAIR),
        grid=(batch, d // lanes, nchunk),
        in_specs=[tok(0), tok(1), tok(2), tok(None), tok(None), tok(None),
                  pl.BlockSpec((SUBLANES, lanes), lambda b, hg, ci: (0, hg))],
        out_specs=tok(None),
        out_shape=jax.ShapeDtypeStruct((m, d), BF16),
        scratch_shapes=[pltpu.VMEM((lanes // PAIR, PAIR, PAIR), F32)],
        compiler_params=_params("arbitrary", "arbitrary", "arbitrary"),
        name="wkv7_chunk",
    )(rkv, rkv, rkv, wl, al, g, prm)


def _pad_lora(w1, w2):
    r = w1.shape[1]
    rp = -(-r // 128) * 128
    return (jnp.pad(w1, ((0, 0), (0, rp - r))).astype(BF16),
            jnp.pad(w2, ((0, rp - r), (0, 0))).astype(BF16))


def kernel(x, c, ada_w, ada_b, norm_g, pool_w, pool_scale, rwkv_mu, rwkv_wr, rwkv_wk, rwkv_wv,
           rwkv_w0, rwkv_w1, rwkv_w2, rwkv_a0, rwkv_a1, rwkv_a2, rwkv_g1, rwkv_g2, rwkv_kk,
           rwkv_ka, rwkv_rk, rwkv_lnx_g, rwkv_lnx_b, rwkv_wo, ffn_w_up, ffn_conv_w, ffn_conv_b,
           ffn_w_down, final_g):
    b, t, d = x.shape
    m = b * t
    depth = ada_w.shape[0]
    assert depth == 2, "layer 0 is the pooling mixer, layer 1 the RWKV-7 mixer"

    mod = _ada(c, ada_w, ada_b).reshape(depth, b, N_ADA, 1, d)
    sh1, sc1, gt1, sh2, sc2, gt2 = [[mod[l, :, n] for l in range(depth)] for n in range(N_ADA)]
    ng = norm_g.reshape(depth, 2, 1, d)

    def ffn(h, l):
        y = _ffn(h.reshape(m, d), ffn_w_up[l].astype(BF16), ffn_conv_w[l], ffn_conv_b[l],
                 ffn_w_down[l].astype(BF16), t)
        return y.reshape(b, t, d)

    x1, h = _pool_layer(x, ng[0, 0], sc1[0], sh1[0], gt1[0], pool_w[0].astype(BF16),
                        pool_scale[0].reshape(1, d), ng[0, 1], sc2[0], sh2[0])
    y = ffn(h, 0)

    mu = rwkv_mu[0][jnp.array([0, 2, 3, 1, 4, 5])]
    x2, mix = _resnorm_mix(x1, y, gt2[0], ng[1, 0], sc1[1], sh1[1], mu)
    mix = mix.reshape(N_SHIFT_MIX, m, d)
    w_rkv = jnp.stack([rwkv_wr[0], rwkv_wk[0], rwkv_wv[0]]).astype(BF16)
    rkv = _matmul(mix, w_rkv)
    wl = _lora(mix, 3, *_pad_lora(rwkv_w1[0], rwkv_w2[0]), "tanh")
    al = _lora(mix, 4, *_pad_lora(rwkv_a1[0], rwkv_a2[0]), "none")
    gg = _lora(mix, 5, *_pad_lora(rwkv_g1[0], rwkv_g2[0]), "sigmoid")
    prm = jnp.stack([rwkv_w0[0], rwkv_a0[0], rwkv_kk[0], rwkv_ka[0], rwkv_rk[0].reshape(d),
                     rwkv_lnx_g[0], rwkv_lnx_b[0], jnp.zeros((d,), F32)])
    z = _wkv(rkv, wl, al, gg, prm, b, t)
    y = _matmul(z[None], rwkv_wo[0].astype(BF16)[None])[0].reshape(b, t, d)
    x3, h = _resnorm(x2, y, gt1[1], ng[1, 1], sc2[1], sh2[1])
    y = ffn(h, 1)
    return _final(x3, y, gt2[1], final_g.reshape(1, d))
```

```python
import functools

import jax
import jax.numpy as jnp
from jax import lax
from jax.experimental import pallas as pl
from jax.experimental.pallas import tpu as pltpu

F32 = jnp.float32
BF16 = jnp.bfloat16

POOL_WINDOWS = (2, 4, 8, 16)
POOL_HALO = 16
HEAD = 64
PAIR = 2 * HEAD
CHUNK = 64
N_SHIFT_MIX = 6
N_ADA = 6
NORM_EPS = 1e-6
LNX_EPS = 64e-5
SUBLANES = 8
VMEM_LIMIT = 56 * 1024 * 1024
ROWS_POOL = 256
ROWS_NORM = 256
ROWS_MIX = 128
ROWS_FFN = 1024
COLS_FFN = 256
COLS_FFN_OUT = 512
ROWS_LORA = 256
TILE_DENSE = 1024
COLS_ADA = 512
LANES_WKV = 8 * PAIR


def _params(*sem):
    return pltpu.CompilerParams(dimension_semantics=sem, vmem_limit_bytes=VMEM_LIMIT)


def _tile(n, want):
    t = min(n, want)
    assert n % t == 0, (n, want)
    return t


def _ada_kernel(c_ref, w_ref, b_ref, o_ref):
    c = c_ref[...]
    ca = (c * jax.nn.sigmoid(c)).astype(BF16)
    o_ref[...] = jnp.dot(ca, w_ref[...].astype(BF16), preferred_element_type=F32) + b_ref[...]


def _ada(c, ada_w, ada_b):
    depth, d, n = ada_w.shape
    b = c.shape[0]
    cp = jnp.zeros((SUBLANES, d), F32).at[:b].set(c)
    tn = _tile(n, COLS_ADA)
    out = pl.pallas_call(
        _ada_kernel,
        grid=(depth, n // tn),
        in_specs=[
            pl.BlockSpec((SUBLANES, d), lambda l, j: (0, 0)),
            pl.BlockSpec((None, d, tn), lambda l, j: (l, 0, j)),
            pl.BlockSpec((None, 1, tn), lambda l, j: (l, 0, j)),
        ],
        out_specs=pl.BlockSpec((None, SUBLANES, tn), lambda l, j: (l, 0, j)),
        out_shape=jax.ShapeDtypeStruct((depth, SUBLANES, n), F32),
        compiler_params=_params("arbitrary", "arbitrary"),
        name="ada_mod",
    )(cp, ada_w, ada_b.reshape(depth, 1, n))
    return out[:, :b]


def _rms_mod(x, g, sc, sh):
    ms = jnp.mean(x * x, axis=-1, keepdims=True)
    return (x * lax.rsqrt(ms + NORM_EPS)) * g * (1.0 + sc) + sh


def _row_spec(d):
    return pl.BlockSpec((None, 1, d), lambda b, t: (b, 0, 0))


def _par_spec(d):
    return pl.BlockSpec((1, d), lambda b, t: (0, 0))


def _pool_kernel(x_ref, g1_ref, sc1_ref, sh1_ref, gt1_ref, pw_ref, ps_ref,
                 g2_ref, sc2_ref, sh2_ref, x1_ref, h2_ref, carry_ref, *, tt, group):
    t = pl.program_id(1)

    @pl.when(t == 0)
    def _():
        carry_ref[...] = jnp.zeros_like(carry_ref)

    x = x_ref[...]
    h = _rms_mod(x, g1_ref[...], sc1_ref[...], sh1_ref[...])
    pos = (t * tt + lax.broadcasted_iota(jnp.int32, (tt, 1), 0) + 1).astype(F32)
    ys = []
    for gi, win in enumerate(POOL_WINDOWS):
        sl = slice(gi * group, (gi + 1) * group)
        hg = h[:, sl]
        s = jnp.concatenate([carry_ref[:, sl], hg], axis=0)
        shift = 1
        while shift < win:
            s = s + pltpu.roll(s, shift, axis=0)
            shift *= 2
        pooled = s[POOL_HALO:, :] / jnp.minimum(pos, float(win)) - hg
        ys.append(jnp.dot(pooled.astype(BF16), pw_ref[gi], preferred_element_type=F32))
    carry_ref[...] = h[tt - POOL_HALO:, :]
    y = jnp.concatenate(ys, axis=1) * ps_ref[...]
    x1 = x + gt1_ref[...] * y
    x1_ref[...] = x1
    h2_ref[...] = _rms_mod(x1, g2_ref[...], sc2_ref[...], sh2_ref[...]).astype(BF16)


def _pool_layer(x, g1, sc1, sh1, gt1, pool_w, pool_scale, g2, sc2, sh2):
    b, t, d = x.shape
    group = d // len(POOL_WINDOWS)
    tt = _tile(t, ROWS_POOL)
    tile = pl.BlockSpec((None, tt, d), lambda bi, ti: (bi, ti, 0))
    return pl.pallas_call(
        functools.partial(_pool_kernel, tt=tt, group=group),
        grid=(b, t // tt),
        in_specs=[tile, _par_spec(d), _row_spec(d), _row_spec(d), _row_spec(d),
                  pl.BlockSpec(pool_w.shape, lambda bi, ti: (0, 0, 0)), _par_spec(d),
                  _par_spec(d), _row_spec(d), _row_spec(d)],
        out_specs=[tile, tile],
        out_shape=[jax.ShapeDtypeStruct((b, t, d), F32), jax.ShapeDtypeStruct((b, t, d), BF16)],
        scratch_shapes=[pltpu.VMEM((POOL_HALO, d), F32)],
        compiler_params=_params("arbitrary", "arbitrary"),
        name="pool_mixer",
    )(x, g1, sc1, sh1, gt1, pool_w, pool_scale, g2, sc2, sh2)


def _gelu(x):
    return 0.5 * x * (1.0 + lax.erf(x * (2.0 ** -0.5)))


def _ffn_kernel(h_ref, wg_ref, wv_ref, cw_ref, cb_ref, wd_ref, y_ref, carry_ref, act_ref,
                *, tm, tn, nf, tiles_per_seq):
    s = pl.program_id(0)

    @pl.when(s == 0)
    def _():
        act_ref[...] = jnp.zeros_like(act_ref)

    act_prev = act_ref[...]
    h = h_ref[...]
    ug = jnp.dot(h, wg_ref[...], preferred_element_type=F32)
    uv = jnp.dot(h, wv_ref[...], preferred_element_type=F32)

    jd = jnp.maximum(s - 1, 0) % nf
    d = y_ref.shape[-1]
    for n in range(d // tn):
        cols = slice(n * tn, (n + 1) * tn)
        part = jnp.dot(act_prev, wd_ref[:, cols], preferred_element_type=F32)
        y_ref[:, cols] = part + jnp.where(jd == 0, 0.0, y_ref[:, cols])

    su = jnp.minimum(s, pl.num_programs(0) - 2)
    i = su // nf
    j = su % nf
    prev = jnp.where(i % tiles_per_seq == 0, 0.0, carry_ref[j])
    carry_ref[j] = ug[tm - SUBLANES:, :]
    ext = jnp.concatenate([prev, ug], axis=0)
    u1 = ext[SUBLANES - 1:SUBLANES - 1 + tm, :]
    u2 = ext[SUBLANES - 2:SUBLANES - 2 + tm, :]
    cw = cw_ref[...]
    gate = ug * cw[2:3, :] + u1 * cw[1:2, :] + u2 * cw[0:1, :] + cb_ref[...]
    act_ref[...] = (_gelu(gate) * uv).astype(BF16)


def _ffn(h, w_up, conv_w, conv_b, w_down, layer, seq_len):
    m, d = h.shape
    f = w_down.shape[1]
    tm = _tile(seq_len, ROWS_FFN)
    tf = _tile(f, COLS_FFN)
    nf = f // tf
    last = (m // tm) * nf - 1

    def up(s):
        return jnp.minimum(s, last)

    def down(s):
        return jnp.maximum(s - 1, 0)

    return pl.pallas_call(
        functools.partial(_ffn_kernel, tm=tm, tn=_tile(d, COLS_FFN_OUT), nf=nf, tiles_per_seq=seq_len // tm),
        grid=(last + 2,),
        in_specs=[
            pl.BlockSpec((tm, d), lambda s: (up(s) // nf, 0), pipeline_mode=pl.Buffered(1)),
            pl.BlockSpec((None, d, tf), lambda s: (layer, 0, up(s) % nf)),
            pl.BlockSpec((None, d, tf), lambda s: (layer, 0, up(s) % nf + nf)),
            pl.BlockSpec((None, conv_w.shape[1], tf), lambda s: (layer, 0, up(s) % nf)),
            pl.BlockSpec((None, 1, tf), lambda s: (layer, 0, up(s) % nf)),
            pl.BlockSpec((None, tf, d), lambda s: (layer, down(s) % nf, 0)),
        ],
        out_specs=pl.BlockSpec((tm, d), lambda s: (down(s) // nf, 0), pipeline_mode=pl.Buffered(1)),
        out_shape=jax.ShapeDtypeStruct((m, d), F32),
        scratch_shapes=[pltpu.VMEM((nf, SUBLANES, tf), F32), pltpu.VMEM((tm, tf), BF16)],
        compiler_params=_params("arbitrary"),
        name="conv_glu",
    )(h, w_up, w_up, conv_w, conv_b, w_down)


def _resnorm_kernel(x_ref, y_ref, gt_ref, g_ref, sc_ref, sh_ref, xo_ref, h_ref):
    x = x_ref[...] + gt_ref[...] * y_ref[...]
    xo_ref[...] = x
    h_ref[...] = _rms_mod(x, g_ref[...], sc_ref[...], sh_ref[...]).astype(h_ref.dtype)


def _resnorm(x, y, gt, g, sc, sh):
    b, t, d = x.shape
    tt = _tile(t, ROWS_NORM)
    tile = pl.BlockSpec((None, tt, d), lambda bi, ti: (bi, ti, 0))
    return pl.pallas_call(
        _resnorm_kernel,
        grid=(b, t // tt),
        in_specs=[tile, tile, _row_spec(d), _par_spec(d), _row_spec(d), _row_spec(d)],
        out_specs=[tile, tile],
        out_shape=[jax.ShapeDtypeStruct((b, t, d), F32), jax.ShapeDtypeStruct((b, t, d), BF16)],
        compiler_params=_params("arbitrary", "arbitrary"),
        name="residual_norm",
    )(x, y, gt, g, sc, sh)


def _resnorm_mix_kernel(x_ref, y_ref, gt_ref, g_ref, sc_ref, sh_ref, mu_ref, xo_ref, mix_ref,
                        carry_ref, *, tt):
    t = pl.program_id(1)

    @pl.when(t == 0)
    def _():
        carry_ref[...] = jnp.zeros_like(carry_ref)

    x = x_ref[...] + gt_ref[...] * y_ref[...]
    xo_ref[...] = x
    h = _rms_mod(x, g_ref[...], sc_ref[...], sh_ref[...])
    first_row = lax.broadcasted_iota(jnp.int32, (tt, 1), 0) == 0
    prev = jnp.where(first_row, carry_ref[SUBLANES - 1:SUBLANES, :], pltpu.roll(h, 1, axis=0))
    carry_ref[...] = h[tt - SUBLANES:, :]
    dx = prev - h
    mu = mu_ref[...]
    for n in range(N_SHIFT_MIX):
        mix_ref[n] = (h + dx * mu[n:n + 1, :]).astype(BF16)


def _resnorm_mix(x, y, gt, g, sc, sh, mu):
    b, t, d = x.shape
    tt = _tile(t, ROWS_MIX)
    tile = pl.BlockSpec((None, tt, d), lambda bi, ti: (bi, ti, 0))
    return pl.pallas_call(
        functools.partial(_resnorm_mix_kernel, tt=tt),
        grid=(b, t // tt),
        in_specs=[tile, tile, _row_spec(d), _par_spec(d), _row_spec(d), _row_spec(d),
                  pl.BlockSpec(mu.shape, lambda bi, ti: (0, 0))],
        out_specs=[tile, pl.BlockSpec((N_SHIFT_MIX, None, tt, d), lambda bi, ti: (0, bi, ti, 0))],
        out_shape=[jax.ShapeDtypeStruct((b, t, d), F32),
                   jax.ShapeDtypeStruct((N_SHIFT_MIX, b, t, d), BF16)],
        scratch_shapes=[pltpu.VMEM((SUBLANES, d), F32)],
        compiler_params=_params("arbitrary", "arbitrary"),
        name="residual_norm_mix",
    )(x, y, gt, g, sc, sh, mu)


def _final_kernel(x_ref, y_ref, gt_ref, g_ref, o_ref):
    x = x_ref[...] + gt_ref[...] * y_ref[...]
    ms = jnp.mean(x * x, axis=-1, keepdims=True)
    o_ref[...] = (x * lax.rsqrt(ms + NORM_EPS)) * g_ref[...]


def _final(x, y, gt, g):
    b, t, d = x.shape
    tt = _tile(t, ROWS_NORM)
    tile = pl.BlockSpec((None, tt, d), lambda bi, ti: (bi, ti, 0))
    return pl.pallas_call(
        _final_kernel,
        grid=(b, t // tt),
        in_specs=[tile, tile, _row_spec(d), _par_spec(d)],
        out_specs=tile,
        out_shape=jax.ShapeDtypeStruct((b, t, d), F32),
        compiler_params=_params("arbitrary", "arbitrary"),
        name="final_norm",
    )(x, y, gt, g)


def _matmul_kernel(a_ref, w_ref, o_ref):
    o_ref[...] = jnp.dot(a_ref[...], w_ref[...], preferred_element_type=F32)


def _matmul(a, w):
    _, m, k = a.shape
    g, _, n = w.shape
    tm = _tile(m, TILE_DENSE)
    tn = _tile(n, TILE_DENSE)
    return pl.pallas_call(
        _matmul_kernel,
        grid=(g, m // tm, n // tn),
        in_specs=[pl.BlockSpec((None, tm, k), lambda gi, i, j: (gi, i, 0)),
                  pl.BlockSpec((None, k, tn), lambda gi, i, j: (gi, 0, j))],
        out_specs=pl.BlockSpec((None, tm, tn), lambda gi, i, j: (gi, i, j)),
        out_shape=jax.ShapeDtypeStruct((g, m, n), F32),
        compiler_params=_params("arbitrary", "arbitrary", "arbitrary"),
        name="dense",
    )(a, w)


def _lora_kernel(x_ref, w1_ref, w2_ref, o_ref, *, act):
    t = jnp.dot(x_ref[...], w1_ref[...], preferred_element_type=F32)
    if act == "tanh":
        t = jnp.tanh(t)
    elif act == "sigmoid":
        t = jax.nn.sigmoid(t)
    o_ref[...] = jnp.dot(t.astype(BF16), w2_ref[...], preferred_element_type=F32)


def _lora(mix, which, w1, w2, act):
    _, m, d = mix.shape
    r = w1.shape[1]
    tm = _tile(m, ROWS_LORA)
    return pl.pallas_call(
        functools.partial(_lora_kernel, act=act),
        grid=(m // tm,),
        in_specs=[pl.BlockSpec((None, tm, d), lambda i: (which, i, 0)),
                  pl.BlockSpec((d, r), lambda i: (0, 0)),
                  pl.BlockSpec((r, d), lambda i: (0, 0))],
        out_specs=pl.BlockSpec((tm, d), lambda i: (i, 0)),
        out_shape=jax.ShapeDtypeStruct((m, d), F32),
        compiler_params=_params("arbitrary"),
        name="lora_" + act,
    )(mix, w1, w2)


def _split(x):
    hi = x.astype(BF16)
    lo = (x - hi.astype(F32)).astype(BF16)
    return hi, lo


def _dg(a, b, ca, cb):
    return lax.dot_general(a, b, (((ca,), (cb,)), ((), ())), preferred_element_type=F32)


def _mm3(a, b, ca, cb):
    a_hi, a_lo = _split(a)
    b_hi, b_lo = _split(b)
    return _dg(jnp.concatenate([a_hi, a_hi, a_lo], axis=ca),
               jnp.concatenate([b_hi, b_lo, b_hi], axis=cb), ca, cb)


def _mm1(a, b, ca, cb):
    return _dg(a.astype(BF16), b.astype(BF16), ca, cb)


def _wkv_chunk(rs, ks, vs, wls, als, gs, prms, s0s, c):
    L = rs[0].shape[0]
    P = range(len(rs))
    w0, a0, kkp, kap, rkp, lng, lnb = [[prm[i:i + 1, :] for prm in prms] for i in range(7)]

    def stack(x):
        return jnp.concatenate([jnp.where(c["head0"], x, 0.0), jnp.where(c["head0"], 0.0, x)], axis=0)

    def headsum(x):
        hi, lo = _split(x)
        return _dg(jnp.concatenate([hi, lo], axis=1), c["ones2"], 1, 0)

    def cumsum(x):
        hi, lo = _split(x)
        return _dg(c["tril2"], jnp.concatenate([hi, lo], axis=0), 1, 0)

    def softplus(z):
        return jnp.maximum(z, 0.0) + jnp.log1p(jnp.exp(-jnp.abs(z)))

    logd = [-jnp.exp(-softplus(-(w0[p] + wls[p])) - 0.5) for p in P]
    lr = [jax.nn.sigmoid(a0[p] + als[p]) for p in P]
    kkr = [ks[p] * kkp[p] for p in P]
    k2 = [ks[p] * (1.0 + (lr[p] - 1.0) * kap[p]) for p in P]
    hs = [headsum(jnp.concatenate([kkr[p] * kkr[p], rs[p] * k2[p] * rkp[p]], axis=0)) for p in P]
    cum = [cumsum(logd[p]) for p in P]
    kk = [kkr[p] / jnp.maximum(jnp.sqrt(hs[p][:L]), 1e-12) for p in P]
    bonus = [hs[p][L:] for p in P]
    p_t = [jnp.exp(cum[p]) for p in P]
    inv_p = [jnp.exp(-cum[p]) for p in P]
    rt = [rs[p] * p_t[p] for p in P]
    at = [-kk[p] * jnp.exp(cum[p] - logd[p]) for p in P]
    bt = [kk[p] * lr[p] * inv_p[p] for p in P]
    kt = [k2[p] * inv_p[p] for p in P]

    bks = [jnp.concatenate([stack(bt[p]), stack(kt[p])], axis=0) for p in P]
    aa = [_mm3(at[p], bks[p], 1, 1) for p in P]
    mm = [_mm1(rt[p], bks[p], 1, 1) for p in P]
    n = [jnp.where(c["strict"], aa[p][:, :PAIR], 0.0) for p in P]
    aak = [jnp.where(c["strict"], aa[p][:, PAIR:], 0.0) for p in P]
    mrbk = [jnp.concatenate([jnp.where(c["incl"], mm[p][:, :PAIR], 0.0),
                             jnp.where(c["incl"], mm[p][:, PAIR:], 0.0)], axis=1) for p in P]
    akv = [_mm3(aak[p], stack(vs[p]), 1, 0) for p in P]

    tinv = [c["eye"] + n[p] for p in P]
    npow = n
    span = 2
    while span < L:
        npow = [_mm3(npow[p], stack(npow[p]), 1, 0) for p in P]
        tinv = [tinv[p] + _mm3(tinv[p], stack(npow[p]), 1, 0) for p in P]
        span *= 2

    tu = [_mm3(tinv[p], jnp.concatenate([stack(at[p]), stack(akv[p])], axis=1), 1, 0) for p in P]
    u = [_mm3(tu[p][:, :PAIR], s0s[p], 1, 1) + tu[p][:, PAIR:] for p in P]
    p_last = [p_t[p][L - 1:L, :] for p in P]
    s_new = [jnp.where(c["blockdiag"],
                       s0s[p] * p_last[p] + _mm3(jnp.concatenate([u[p], vs[p]], axis=0),
                                                 jnp.concatenate([bt[p], kt[p]], axis=0) * p_last[p], 0, 0),
                       0.0) for p in P]
    y = [_mm1(rt[p], s0s[p], 1, 1)
         + _mm1(mrbk[p], jnp.concatenate([stack(u[p]), stack(vs[p])], axis=0), 1, 0) for p in P]

    mean = [headsum(y[p]) * (1.0 / HEAD) for p in P]
    dlt = [y[p] - mean[p] for p in P]
    var = [headsum(dlt[p] * dlt[p]) * (1.0 / HEAD) for p in P]
    outs = [((dlt[p] * lax.rsqrt(var[p] + LNX_EPS) * lng[p] + lnb[p]) + bonus[p] * vs[p]) * gs[p] for p in P]
    return outs, s_new


def _wkv_consts(L):
    lane = lax.broadcasted_iota(jnp.int32, (L, PAIR), 1)
    row = lax.broadcasted_iota(jnp.int32, (L, PAIR), 0)
    src = lane % HEAD
    r2 = lax.broadcasted_iota(jnp.int32, (PAIR, PAIR), 0) // HEAD
    c2 = lax.broadcasted_iota(jnp.int32, (PAIR, PAIR), 1) // HEAD
    blockdiag = r2 == c2
    ones = jnp.where(blockdiag, 1.0, 0.0).astype(BF16)
    tr = lax.broadcasted_iota(jnp.int32, (L, L), 0)
    tc = lax.broadcasted_iota(jnp.int32, (L, L), 1)
    tril = jnp.where(tr >= tc, 1.0, 0.0).astype(BF16)
    return {
        "head0": lane < HEAD,
        "strict": row > src,
        "incl": row >= src,
        "eye": jnp.where(row == src, 1.0, 0.0),
        "blockdiag": blockdiag,
        "ones2": jnp.concatenate([ones, ones], axis=0),
        "tril2": jnp.concatenate([tril, tril], axis=1),
    }


def _wkv_kernel(r_ref, k_ref, v_ref, wl_ref, al_ref, g_ref, prm_ref, o_ref, s_ref, *, pairs):
    @pl.when(pl.program_id(2) == 0)
    def _():
        s_ref[...] = jnp.zeros_like(s_ref)

    c = _wkv_consts(r_ref.shape[0])
    cols = [slice(p * PAIR, (p + 1) * PAIR) for p in range(pairs)]
    outs, s_new = _wkv_chunk(*[[ref[:, cs] for cs in cols]
                               for ref in (r_ref, k_ref, v_ref, wl_ref, al_ref, g_ref, prm_ref)],
                             [s_ref[p] for p in range(pairs)], c)
    for p in range(pairs):
        s_ref[p] = s_new[p]
        o_ref[:, cols[p]] = outs[p].astype(o_ref.dtype)


def _wkv(rkv, wl, al, g, prm, batch, seq_len):
    _, m, d = rkv.shape
    L = _tile(seq_len, CHUNK)
    lanes = _tile(d, LANES_WKV)
    nchunk = seq_len // L

    def tok(which):
        if which is None:
            return pl.BlockSpec((L, lanes), lambda b, hg, ci: (b * nchunk + ci, hg))
        return pl.BlockSpec((None, L, lanes), lambda b, hg, ci: (which, b * nchunk + ci, hg))

    return pl.pallas_call(
        functools.partial(_wkv_kernel, pairs=lanes // PAIR),
        grid=(batch, d // lanes, nchunk),
        in_specs=[tok(0), tok(1), tok(2), tok(None), tok(None), tok(None),
                  pl.BlockSpec((SUBLANES, lanes), lambda b, hg, ci: (0, hg))],
        out_specs=tok(None),
        out_shape=jax.ShapeDtypeStruct((m, d), BF16),
        scratch_shapes=[pltpu.VMEM((lanes // PAIR, PAIR, PAIR), F32)],
        compiler_params=_params("arbitrary", "arbitrary", "arbitrary"),
        name="wkv7_chunk",
    )(rkv, rkv, rkv, wl, al, g, prm)


def _pad_lora(w1, w2):
    r = w1.shape[1]
    rp = -(-r // 128) * 128
    return (jnp.pad(w1, ((0, 0), (0, rp - r))).astype(BF16),
            jnp.pad(w2, ((0, rp - r), (0, 0))).astype(BF16))


def kernel(x, c, ada_w, ada_b, norm_g, pool_w, pool_scale, rwkv_mu, rwkv_wr, rwkv_wk, rwkv_wv,
           rwkv_w0, rwkv_w1, rwkv_w2, rwkv_a0, rwkv_a1, rwkv_a2, rwkv_g1, rwkv_g2, rwkv_kk,
           rwkv_ka, rwkv_rk, rwkv_lnx_g, rwkv_lnx_b, rwkv_wo, ffn_w_up, ffn_conv_w, ffn_conv_b,
           ffn_w_down, final_g):
    b, t, d = x.shape
    m = b * t
    depth = ada_w.shape[0]
    assert depth == 2, "layer 0 is the pooling mixer, layer 1 the RWKV-7 mixer"

    mod = _ada(c, ada_w, ada_b).reshape(depth, b, N_ADA, 1, d)
    sh1, sc1, gt1, sh2, sc2, gt2 = [[mod[l, :, n] for l in range(depth)] for n in range(N_ADA)]
    ng = norm_g.reshape(depth, 2, 1, d)

    w_up = ffn_w_up.astype(BF16)
    w_down = ffn_w_down.astype(BF16)
    conv_b = ffn_conv_b.reshape(depth, 1, -1)

    def ffn(h, l):
        return _ffn(h.reshape(m, d), w_up, ffn_conv_w, conv_b, w_down, l, t).reshape(b, t, d)

    x1, h = _pool_layer(x, ng[0, 0], sc1[0], sh1[0], gt1[0], pool_w[0].astype(BF16),
                        pool_scale[0].reshape(1, d), ng[0, 1], sc2[0], sh2[0])
    y = ffn(h, 0)

    mu = rwkv_mu[0][jnp.array([0, 2, 3, 1, 4, 5])]
    x2, mix = _resnorm_mix(x1, y, gt2[0], ng[1, 0], sc1[1], sh1[1], mu)
    mix = mix.reshape(N_SHIFT_MIX, m, d)
    w_rkv = jnp.stack([rwkv_wr[0], rwkv_wk[0], rwkv_wv[0]]).astype(BF16)
    rkv = _matmul(mix, w_rkv)
    wl = _lora(mix, 3, *_pad_lora(rwkv_w1[0], rwkv_w2[0]), "tanh")
    al = _lora(mix, 4, *_pad_lora(rwkv_a1[0], rwkv_a2[0]), "none")
    gg = _lora(mix, 5, *_pad_lora(rwkv_g1[0], rwkv_g2[0]), "sigmoid")
    prm = jnp.stack([rwkv_w0[0], rwkv_a0[0], rwkv_kk[0], rwkv_ka[0], rwkv_rk[0].reshape(d),
                     rwkv_lnx_g[0], rwkv_lnx_b[0], jnp.zeros((d,), F32)])
    z = _wkv(rkv, wl, al, gg, prm, b, t)
    y = _matmul(z[None], rwkv_wo[0].astype(BF16)[None])[0].reshape(b, t, d)
    x3, h = _resnorm(x2, y, gt1[1], ng[1, 1], sc2[1], sh2[1])
    y = ffn(h, 1)
    return _final(x3, y, gt2[1], final_g.reshape(1, d))
```

```python
import functools

import jax
import jax.numpy as jnp
from jax import lax
from jax.experimental import pallas as pl
from jax.experimental.pallas import tpu as pltpu

F32 = jnp.float32
BF16 = jnp.bfloat16

POOL_WINDOWS = (2, 4, 8, 16)
POOL_HALO = 16
HEAD = 64
PAIR = 2 * HEAD
CHUNK = 64
INV_BASE = 8
N_SHIFT_MIX = 6
N_ADA = 6
NORM_EPS = 1e-6
LNX_EPS = 64e-5
SUBLANES = 8
VMEM_LIMIT = 60000 * 1024
ROWS_POOL = 256
ROWS_NORM = 256
ROWS_MIX = 128
ROWS_FFN = 1024
COLS_FFN = 512
COLS_FFN_UP = 256
COLS_FFN_OUT = 512
ROWS_LORA = 256
TILE_DENSE = 1024
COLS_ADA = 512
LANES_WKV = 16 * PAIR


def _params(*sem):
    return pltpu.CompilerParams(dimension_semantics=sem, vmem_limit_bytes=VMEM_LIMIT)


def _tile(n, want):
    t = min(n, want)
    assert n % t == 0, (n, want)
    return t


def _ada_kernel(c_ref, w_ref, b_ref, o_ref):
    c = c_ref[...]
    ca = (c * jax.nn.sigmoid(c)).astype(BF16)
    o_ref[...] = jnp.dot(ca, w_ref[...].astype(BF16), preferred_element_type=F32) + b_ref[...]


def _ada(c, ada_w, ada_b):
    depth, d, n = ada_w.shape
    b = c.shape[0]
    cp = jnp.zeros((SUBLANES, d), F32).at[:b].set(c)
    tn = _tile(n, COLS_ADA)
    out = pl.pallas_call(
        _ada_kernel,
        grid=(depth, n // tn),
        in_specs=[
            pl.BlockSpec((SUBLANES, d), lambda l, j: (0, 0)),
            pl.BlockSpec((None, d, tn), lambda l, j: (l, 0, j)),
            pl.BlockSpec((None, 1, tn), lambda l, j: (l, 0, j)),
        ],
        out_specs=pl.BlockSpec((None, SUBLANES, tn), lambda l, j: (l, 0, j)),
        out_shape=jax.ShapeDtypeStruct((depth, SUBLANES, n), F32),
        compiler_params=_params("arbitrary", "arbitrary"),
        name="ada_mod",
    )(cp, ada_w, ada_b.reshape(depth, 1, n))
    return out[:, :b]


def _rms_mod(x, g, sc, sh):
    ms = jnp.mean(x * x, axis=-1, keepdims=True)
    return (x * lax.rsqrt(ms + NORM_EPS)) * g * (1.0 + sc) + sh


def _row_spec(d):
    return pl.BlockSpec((None, 1, d), lambda b, t: (b, 0, 0))


def _par_spec(d):
    return pl.BlockSpec((1, d), lambda b, t: (0, 0))


def _pool_kernel(x_ref, g1_ref, sc1_ref, sh1_ref, gt1_ref, pw_ref, ps_ref,
                 g2_ref, sc2_ref, sh2_ref, x1_ref, h2_ref, carry_ref, *, tt, group):
    t = pl.program_id(1)

    @pl.when(t == 0)
    def _():
        carry_ref[...] = jnp.zeros_like(carry_ref)

    x = x_ref[...]
    h = _rms_mod(x, g1_ref[...], sc1_ref[...], sh1_ref[...])
    pos = (t * tt + lax.broadcasted_iota(jnp.int32, (tt, 1), 0) + 1).astype(F32)
    ys = []
    for gi, win in enumerate(POOL_WINDOWS):
        sl = slice(gi * group, (gi + 1) * group)
        hg = h[:, sl]
        s = jnp.concatenate([carry_ref[:, sl], hg], axis=0)
        shift = 1
        while shift < win:
            s = s + pltpu.roll(s, shift, axis=0)
            shift *= 2
        pooled = s[POOL_HALO:, :] / jnp.minimum(pos, float(win)) - hg
        ys.append(jnp.dot(pooled.astype(BF16), pw_ref[gi], preferred_element_type=F32))
    carry_ref[...] = h[tt - POOL_HALO:, :]
    y = jnp.concatenate(ys, axis=1) * ps_ref[...]
    x1 = x + gt1_ref[...] * y
    x1_ref[...] = x1
    h2_ref[...] = _rms_mod(x1, g2_ref[...], sc2_ref[...], sh2_ref[...]).astype(BF16)


def _pool_layer(x, g1, sc1, sh1, gt1, pool_w, pool_scale, g2, sc2, sh2):
    b, t, d = x.shape
    group = d // len(POOL_WINDOWS)
    tt = _tile(t, ROWS_POOL)
    tile = pl.BlockSpec((None, tt, d), lambda bi, ti: (bi, ti, 0))
    return pl.pallas_call(
        functools.partial(_pool_kernel, tt=tt, group=group),
        grid=(b, t // tt),
        in_specs=[tile, _par_spec(d), _row_spec(d), _row_spec(d), _row_spec(d),
                  pl.BlockSpec(pool_w.shape, lambda bi, ti: (0, 0, 0)), _par_spec(d),
                  _par_spec(d), _row_spec(d), _row_spec(d)],
        out_specs=[tile, tile],
        out_shape=[jax.ShapeDtypeStruct((b, t, d), F32), jax.ShapeDtypeStruct((b, t, d), BF16)],
        scratch_shapes=[pltpu.VMEM((POOL_HALO, d), F32)],
        compiler_params=_params("arbitrary", "arbitrary"),
        name="pool_mixer",
    )(x, g1, sc1, sh1, gt1, pool_w, pool_scale, g2, sc2, sh2)


def _gelu(x):
    return 0.5 * x * (1.0 + lax.erf(x * (2.0 ** -0.5)))


def _ffn_kernel(h_ref, wg_ref, wv_ref, cw_ref, cb_ref, wd_ref, y_ref, carry_ref, act_ref,
                *, tm, tn, nf, tiles_per_seq):
    s = pl.program_id(0)

    @pl.when(s == 0)
    def _():
        act_ref[...] = jnp.zeros_like(act_ref)

    act_prev = act_ref[...]
    jd = jnp.maximum(s - 1, 0) % nf
    d = y_ref.shape[-1]
    for n in range(d // tn):
        cols = slice(n * tn, (n + 1) * tn)
        part = jnp.dot(act_prev, wd_ref[:, cols], preferred_element_type=F32)
        y_ref[:, cols] = part + jnp.where(jd == 0, 0.0, y_ref[:, cols])

    su = jnp.minimum(s, pl.num_programs(0) - 2)
    first = (su // nf) % tiles_per_seq == 0
    j = su % nf
    h = h_ref[...]
    tf = act_ref.shape[-1]
    tu = min(tf, COLS_FFN_UP)
    for n in range(tf // tu):
        cols = slice(n * tu, (n + 1) * tu)
        ug = jnp.dot(h, wg_ref[:, cols], preferred_element_type=F32)
        uv = jnp.dot(h, wv_ref[:, cols], preferred_element_type=F32)
        prev = jnp.where(first, 0.0, carry_ref[j, :, cols])
        carry_ref[j, :, cols] = ug[tm - SUBLANES:, :]
        ext = jnp.concatenate([prev, ug], axis=0)
        u1 = ext[SUBLANES - 1:SUBLANES - 1 + tm, :]
        u2 = ext[SUBLANES - 2:SUBLANES - 2 + tm, :]
        gate = (ug * cw_ref[2:3, cols] + u1 * cw_ref[1:2, cols] + u2 * cw_ref[0:1, cols] + cb_ref[:, cols])
        act_ref[:, cols] = (_gelu(gate) * uv).astype(BF16)


def _ffn(h, w_up, conv_w, conv_b, w_down, layer, seq_len):
    m, d = h.shape
    f = w_down.shape[1]
    tm = _tile(seq_len, ROWS_FFN)
    tf = _tile(f, COLS_FFN)
    nf = f // tf
    last = (m // tm) * nf - 1

    def up(s):
        return jnp.minimum(s, last)

    def down(s):
        return jnp.maximum(s - 1, 0)

    return pl.pallas_call(
        functools.partial(_ffn_kernel, tm=tm, tn=_tile(d, COLS_FFN_OUT), nf=nf, tiles_per_seq=seq_len // tm),
        grid=(last + 2,),
        in_specs=[
            pl.BlockSpec((tm, d), lambda s: (up(s) // nf, 0), pipeline_mode=pl.Buffered(1)),
            pl.BlockSpec((None, d, tf), lambda s: (layer, 0, up(s) % nf)),
            pl.BlockSpec((None, d, tf), lambda s: (layer, 0, up(s) % nf + nf)),
            pl.BlockSpec((None, conv_w.shape[1], tf), lambda s: (layer, 0, up(s) % nf)),
            pl.BlockSpec((None, 1, tf), lambda s: (layer, 0, up(s) % nf)),
            pl.BlockSpec((None, tf, d), lambda s: (layer, down(s) % nf, 0)),
        ],
        out_specs=pl.BlockSpec((tm, d), lambda s: (down(s) // nf, 0), pipeline_mode=pl.Buffered(1)),
        out_shape=jax.ShapeDtypeStruct((m, d), F32),
        scratch_shapes=[pltpu.VMEM((nf, SUBLANES, tf), F32), pltpu.VMEM((tm, tf), BF16)],
        compiler_params=_params("arbitrary"),
        name="conv_glu",
    )(h, w_up, w_up, conv_w, conv_b, w_down)


def _resnorm_kernel(x_ref, y_ref, gt_ref, g_ref, sc_ref, sh_ref, xo_ref, h_ref):
    x = x_ref[...] + gt_ref[...] * y_ref[...]
    xo_ref[...] = x
    h_ref[...] = _rms_mod(x, g_ref[...], sc_ref[...], sh_ref[...]).astype(h_ref.dtype)


def _resnorm(x, y, gt, g, sc, sh):
    b, t, d = x.shape
    tt = _tile(t, ROWS_NORM)
    tile = pl.BlockSpec((None, tt, d), lambda bi, ti: (bi, ti, 0))
    return pl.pallas_call(
        _resnorm_kernel,
        grid=(b, t // tt),
        in_specs=[tile, tile, _row_spec(d), _par_spec(d), _row_spec(d), _row_spec(d)],
        out_specs=[tile, tile],
        out_shape=[jax.ShapeDtypeStruct((b, t, d), F32), jax.ShapeDtypeStruct((b, t, d), BF16)],
        compiler_params=_params("arbitrary", "arbitrary"),
        name="residual_norm",
    )(x, y, gt, g, sc, sh)


def _resnorm_mix_kernel(x_ref, y_ref, gt_ref, g_ref, sc_ref, sh_ref, mu_ref, xo_ref, mix_ref,
                        carry_ref, *, tt):
    t = pl.program_id(1)

    @pl.when(t == 0)
    def _():
        carry_ref[...] = jnp.zeros_like(carry_ref)

    x = x_ref[...] + gt_ref[...] * y_ref[...]
    xo_ref[...] = x
    h = _rms_mod(x, g_ref[...], sc_ref[...], sh_ref[...])
    first_row = lax.broadcasted_iota(jnp.int32, (tt, 1), 0) == 0
    prev = jnp.where(first_row, carry_ref[SUBLANES - 1:SUBLANES, :], pltpu.roll(h, 1, axis=0))
    carry_ref[...] = h[tt - SUBLANES:, :]
    dx = prev - h
    mu = mu_ref[...]
    for n in range(N_SHIFT_MIX):
        mix_ref[n] = (h + dx * mu[n:n + 1, :]).astype(BF16)


def _resnorm_mix(x, y, gt, g, sc, sh, mu):
    b, t, d = x.shape
    tt = _tile(t, ROWS_MIX)
    tile = pl.BlockSpec((None, tt, d), lambda bi, ti: (bi, ti, 0))
    return pl.pallas_call(
        functools.partial(_resnorm_mix_kernel, tt=tt),
        grid=(b, t // tt),
        in_specs=[tile, tile, _row_spec(d), _par_spec(d), _row_spec(d), _row_spec(d),
                  pl.BlockSpec(mu.shape, lambda bi, ti: (0, 0))],
        out_specs=[tile, pl.BlockSpec((N_SHIFT_MIX, None, tt, d), lambda bi, ti: (0, bi, ti, 0))],
        out_shape=[jax.ShapeDtypeStruct((b, t, d), F32),
                   jax.ShapeDtypeStruct((N_SHIFT_MIX, b, t, d), BF16)],
        scratch_shapes=[pltpu.VMEM((SUBLANES, d), F32)],
        compiler_params=_params("arbitrary", "arbitrary"),
        name="residual_norm_mix",
    )(x, y, gt, g, sc, sh, mu)


def _final_kernel(x_ref, y_ref, gt_ref, g_ref, o_ref):
    x = x_ref[...] + gt_ref[...] * y_ref[...]
    ms = jnp.mean(x * x, axis=-1, keepdims=True)
    o_ref[...] = (x * lax.rsqrt(ms + NORM_EPS)) * g_ref[...]


def _final(x, y, gt, g):
    b, t, d = x.shape
    tt = _tile(t, ROWS_NORM)
    tile = pl.BlockSpec((None, tt, d), lambda bi, ti: (bi, ti, 0))
    return pl.pallas_call(
        _final_kernel,
        grid=(b, t // tt),
        in_specs=[tile, tile, _row_spec(d), _par_spec(d)],
        out_specs=tile,
        out_shape=jax.ShapeDtypeStruct((b, t, d), F32),
        compiler_params=_params("arbitrary", "arbitrary"),
        name="final_norm",
    )(x, y, gt, g)


def _matmul_kernel(a_ref, w_ref, o_ref):
    o_ref[...] = jnp.dot(a_ref[...], w_ref[...], preferred_element_type=F32)


def _matmul(a, w):
    _, m, k = a.shape
    g, _, n = w.shape
    tm = _tile(m, TILE_DENSE)
    tn = _tile(n, TILE_DENSE)
    return pl.pallas_call(
        _matmul_kernel,
        grid=(g, m // tm, n // tn),
        in_specs=[pl.BlockSpec((None, tm, k), lambda gi, i, j: (gi, i, 0)),
                  pl.BlockSpec((None, k, tn), lambda gi, i, j: (gi, 0, j))],
        out_specs=pl.BlockSpec((None, tm, tn), lambda gi, i, j: (gi, i, j)),
        out_shape=jax.ShapeDtypeStruct((g, m, n), F32),
        compiler_params=_params("arbitrary", "arbitrary", "arbitrary"),
        name="dense",
    )(a, w)


def _lora_kernel(x_ref, w1_ref, w2_ref, o_ref, *, act):
    t = jnp.dot(x_ref[...], w1_ref[...], preferred_element_type=F32)
    if act == "tanh":
        t = jnp.tanh(t)
    elif act == "sigmoid":
        t = jax.nn.sigmoid(t)
    o_ref[...] = jnp.dot(t.astype(BF16), w2_ref[...], preferred_element_type=F32)


def _lora(mix, which, w1, w2, act):
    _, m, d = mix.shape
    r = w1.shape[1]
    tm = _tile(m, ROWS_LORA)
    return pl.pallas_call(
        functools.partial(_lora_kernel, act=act),
        grid=(m // tm,),
        in_specs=[pl.BlockSpec((None, tm, d), lambda i: (which, i, 0)),
                  pl.BlockSpec((d, r), lambda i: (0, 0)),
                  pl.BlockSpec((r, d), lambda i: (0, 0))],
        out_specs=pl.BlockSpec((tm, d), lambda i: (i, 0)),
        out_shape=jax.ShapeDtypeStruct((m, d), F32),
        compiler_params=_params("arbitrary"),
        name="lora_" + act,
    )(mix, w1, w2)


def _split(x):
    hi = x.astype(BF16)
    lo = (x - hi.astype(F32)).astype(BF16)
    return hi, lo


def _dg(a, b, ca, cb):
    return lax.dot_general(a, b, (((ca,), (cb,)), ((), ())), preferred_element_type=F32)


def _mm3(a, b, ca, cb):
    a_hi, a_lo = _split(a)
    b_hi, b_lo = _split(b)
    return _dg(jnp.concatenate([a_hi, a_hi, a_lo], axis=ca),
               jnp.concatenate([b_hi, b_lo, b_hi], axis=cb), ca, cb)


def _mm1(a, b, ca, cb):
    return _dg(a.astype(BF16), b.astype(BF16), ca, cb)


_mm = _mm1
_mm_state = _mm1


def _wkv_chunk(rs, ks, vs, wls, als, gs, prms, s0s, c):
    L = rs[0].shape[0]
    P = range(len(rs))
    w0, a0, kkp, kap, rkp, lng, lnb = [[prm[i:i + 1, :] for prm in prms] for i in range(7)]

    def stack(x):
        return jnp.concatenate([jnp.where(c["head0"], x, 0.0), jnp.where(c["head0"], 0.0, x)], axis=0)

    def headsum(x):
        hi, lo = _split(x)
        return _dg(jnp.concatenate([hi, lo], axis=1), c["ones2"], 1, 0)

    def cumsum(x):
        hi, lo = _split(x)
        return _dg(c["tril2"], jnp.concatenate([hi, lo], axis=0), 1, 0)

    def softplus(z):
        return jnp.maximum(z, 0.0) + jnp.log1p(jnp.exp(-jnp.abs(z)))

    logd = [-jnp.exp(-softplus(-(w0[p] + wls[p])) - 0.5) for p in P]
    lr = [jax.nn.sigmoid(a0[p] + als[p]) for p in P]
    kkr = [ks[p] * kkp[p] for p in P]
    k2 = [ks[p] * (1.0 + (lr[p] - 1.0) * kap[p]) for p in P]
    hs = [headsum(jnp.concatenate([kkr[p] * kkr[p], rs[p] * k2[p] * rkp[p]], axis=0)) for p in P]
    cum = [cumsum(logd[p]) for p in P]
    kk = [kkr[p] / jnp.maximum(jnp.sqrt(hs[p][:L]), 1e-12) for p in P]
    bonus = [hs[p][L:] for p in P]
    p_t = [jnp.exp(cum[p]) for p in P]
    inv_p = [jnp.exp(-cum[p]) for p in P]
    rt = [rs[p] * p_t[p] for p in P]
    at = [-kk[p] * jnp.exp(cum[p] - logd[p]) for p in P]
    bt = [kk[p] * lr[p] * inv_p[p] for p in P]
    kt = [k2[p] * inv_p[p] for p in P]

    bks = [jnp.concatenate([stack(bt[p]), stack(kt[p])], axis=0) for p in P]
    aa = [_mm(at[p], bks[p], 1, 1) for p in P]
    mm = [_mm1(rt[p], bks[p], 1, 1) for p in P]
    n = [jnp.where(c["strict"], aa[p][:, :PAIR], 0.0) for p in P]
    aak = [jnp.where(c["strict"], aa[p][:, PAIR:], 0.0) for p in P]
    mrbk = [jnp.concatenate([jnp.where(c["incl"], mm[p][:, :PAIR], 0.0),
                             jnp.where(c["incl"], mm[p][:, PAIR:], 0.0)], axis=1) for p in P]
    akv = [_mm(aak[p], stack(vs[p]), 1, 0) for p in P]

    npow = [jnp.where(c["diag"], n[p], 0.0) for p in P]
    tinv = [c["eye"] + npow[p] for p in P]
    span = 2
    while span < INV_BASE:
        npow = [_mm(npow[p], stack(npow[p]), 1, 0) for p in P]
        tinv = [tinv[p] + _mm(tinv[p], stack(npow[p]), 1, 0) for p in P]
        span *= 2
    for sub in c["sub"]:
        low = [_mm(jnp.where(sub, n[p], 0.0), stack(tinv[p]), 1, 0) for p in P]
        tinv = [tinv[p] + _mm(tinv[p], stack(low[p]), 1, 0) for p in P]

    tu = [_mm(tinv[p], jnp.concatenate([stack(at[p]), stack(akv[p])], axis=1), 1, 0) for p in P]
    u = [_mm_state(tu[p][:, :PAIR], s0s[p], 1, 1) + tu[p][:, PAIR:] for p in P]
    p_last = [p_t[p][L - 1:L, :] for p in P]
    s_new = [jnp.where(c["blockdiag"],
                       s0s[p] * p_last[p] + _mm_state(jnp.concatenate([u[p], vs[p]], axis=0),
                                                 jnp.concatenate([bt[p], kt[p]], axis=0) * p_last[p], 0, 0),
                       0.0) for p in P]
    y = [_mm1(rt[p], s0s[p], 1, 1)
         + _mm1(mrbk[p], jnp.concatenate([stack(u[p]), stack(vs[p])], axis=0), 1, 0) for p in P]

    mean = [headsum(y[p]) * (1.0 / HEAD) for p in P]
    dlt = [y[p] - mean[p] for p in P]
    var = [headsum(dlt[p] * dlt[p]) * (1.0 / HEAD) for p in P]
    outs = [((dlt[p] * lax.rsqrt(var[p] + LNX_EPS) * lng[p] + lnb[p]) + bonus[p] * vs[p]) * gs[p] for p in P]
    return outs, s_new


def _wkv_consts(L):
    lane = lax.broadcasted_iota(jnp.int32, (L, PAIR), 1)
    row = lax.broadcasted_iota(jnp.int32, (L, PAIR), 0)
    src = lane % HEAD
    r2 = lax.broadcasted_iota(jnp.int32, (PAIR, PAIR), 0) // HEAD
    c2 = lax.broadcasted_iota(jnp.int32, (PAIR, PAIR), 1) // HEAD
    blockdiag = r2 == c2
    ones = jnp.where(blockdiag, 1.0, 0.0).astype(BF16)
    tr = lax.broadcasted_iota(jnp.int32, (L, L), 0)
    tc = lax.broadcasted_iota(jnp.int32, (L, L), 1)
    tril = jnp.where(tr >= tc, 1.0, 0.0).astype(BF16)
    sub = []
    q = INV_BASE
    while q < L:
        sub.append((row // (2 * q) == src // (2 * q)) & (row // q == src // q + 1))
        q *= 2
    return {
        "head0": lane < HEAD,
        "strict": row > src,
        "incl": row >= src,
        "eye": jnp.where(row == src, 1.0, 0.0),
        "diag": row // INV_BASE == src // INV_BASE,
        "sub": sub,
        "blockdiag": blockdiag,
        "ones2": jnp.concatenate([ones, ones], axis=0),
        "tril2": jnp.concatenate([tril, tril], axis=1),
    }


def _wkv_kernel(r_ref, k_ref, v_ref, wl_ref, al_ref, g_ref, prm_ref, o_ref, s_ref, *, pairs):
    @pl.when(pl.program_id(2) == 0)
    def _():
        s_ref[...] = jnp.zeros_like(s_ref)

    c = _wkv_consts(r_ref.shape[0])
    cols = [slice(p * PAIR, (p + 1) * PAIR) for p in range(pairs)]
    outs, s_new = _wkv_chunk(*[[ref[:, cs] for cs in cols]
                               for ref in (r_ref, k_ref, v_ref, wl_ref, al_ref, g_ref, prm_ref)],
                             [s_ref[p] for p in range(pairs)], c)
    for p in range(pairs):
        s_ref[p] = s_new[p]
        o_ref[:, cols[p]] = outs[p].astype(o_ref.dtype)


def _wkv(rkv, wl, al, g, prm, batch, seq_len):
    _, m, d = rkv.shape
    L = _tile(seq_len, CHUNK)
    lanes = _tile(d, LANES_WKV)
    nchunk = seq_len // L

    def tok(which):
        if which is None:
            return pl.BlockSpec((L, lanes), lambda b, hg, ci: (b * nchunk + ci, hg))
        return pl.BlockSpec((None, L, lanes), lambda b, hg, ci: (which, b * nchunk + ci, hg))

    return pl.pallas_call(
        functools.partial(_wkv_kernel, pairs=lanes // PAIR),
        grid=(batch, d // lanes, nchunk),
        in_specs=[tok(0), tok(1), tok(2), tok(None), tok(None), tok(None),
                  pl.BlockSpec((SUBLANES, lanes), lambda b, hg, ci: (0, hg))],
        out_specs=tok(None),
        out_shape=jax.ShapeDtypeStruct((m, d), BF16),
        scratch_shapes=[pltpu.VMEM((lanes // PAIR, PAIR, PAIR), F32)],
        compiler_params=_params("arbitrary", "arbitrary", "arbitrary"),
        name="wkv7_chunk",
    )(rkv, rkv, rkv, wl, al, g, prm)


def _pad_lora(w1, w2):
    r = w1.shape[1]
    rp = -(-r // 128) * 128
    return (jnp.pad(w1, ((0, 0), (0, rp - r))).astype(BF16),
            jnp.pad(w2, ((0, rp - r), (0, 0))).astype(BF16))


def kernel(x, c, ada_w, ada_b, norm_g, pool_w, pool_scale, rwkv_mu, rwkv_wr, rwkv_wk, rwkv_wv,
           rwkv_w0, rwkv_w1, rwkv_w2, rwkv_a0, rwkv_a1, rwkv_a2, rwkv_g1, rwkv_g2, rwkv_kk,
           rwkv_ka, rwkv_rk, rwkv_lnx_g, rwkv_lnx_b, rwkv_wo, ffn_w_up, ffn_conv_w, ffn_conv_b,
           ffn_w_down, final_g):
    b, t, d = x.shape
    m = b * t
    depth = ada_w.shape[0]
    assert depth == 2, "layer 0 is the pooling mixer, layer 1 the RWKV-7 mixer"

    mod = _ada(c, ada_w, ada_b).reshape(depth, b, N_ADA, 1, d)
    sh1, sc1, gt1, sh2, sc2, gt2 = [[mod[l, :, n] for l in range(depth)] for n in range(N_ADA)]
    ng = norm_g.reshape(depth, 2, 1, d)

    f = ffn_w_down.shape[1]
    fpad = -f % min(f, COLS_FFN)
    w_up = jnp.pad(ffn_w_up.reshape(depth, d, 2, f), ((0, 0), (0, 0), (0, 0), (0, fpad)))
    w_up = w_up.astype(BF16).reshape(depth, d, 2 * (f + fpad))
    w_down = jnp.pad(ffn_w_down, ((0, 0), (0, fpad), (0, 0))).astype(BF16)
    conv_w = jnp.pad(ffn_conv_w, ((0, 0), (0, 0), (0, fpad)))
    conv_b = jnp.pad(ffn_conv_b, ((0, 0), (0, fpad))).reshape(depth, 1, f + fpad)

    def ffn(h, l):
        return _ffn(h.reshape(m, d), w_up, conv_w, conv_b, w_down, l, t).reshape(b, t, d)

    x1, h = _pool_layer(x, ng[0, 0], sc1[0], sh1[0], gt1[0], pool_w[0].astype(BF16),
                        pool_scale[0].reshape(1, d), ng[0, 1], sc2[0], sh2[0])
    y = ffn(h, 0)

    mu = rwkv_mu[0][jnp.array([0, 2, 3, 1, 4, 5])]
    x2, mix = _resnorm_mix(x1, y, gt2[0], ng[1, 0], sc1[1], sh1[1], mu)
    mix = mix.reshape(N_SHIFT_MIX, m, d)
    w_rkv = jnp.stack([rwkv_wr[0], rwkv_wk[0], rwkv_wv[0]]).astype(BF16)
    rkv = _matmul(mix, w_rkv)
    wl = _lora(mix, 3, *_pad_lora(rwkv_w1[0], rwkv_w2[0]), "tanh")
    al = _lora(mix, 4, *_pad_lora(rwkv_a1[0], rwkv_a2[0]), "none")
    gg = _lora(mix, 5, *_pad_lora(rwkv_g1[0], rwkv_g2[0]), "sigmoid")
    prm = jnp.stack([rwkv_w0[0], rwkv_a0[0], rwkv_kk[0], rwkv_ka[0], rwkv_rk[0].reshape(d),
                     rwkv_lnx_g[0], rwkv_lnx_b[0], jnp.zeros((d,), F32)])
    z = _wkv(rkv, wl, al, gg, prm, b, t)
    y = _matmul(z[None], rwkv_wo[0].astype(BF16)[None])[0].reshape(b, t, d)
    x3, h = _resnorm(x2, y, gt1[1], ng[1, 1], sc2[1], sh2[1])
    y = ffn(h, 1)
    return _final(x3, y, gt2[1], final_g.reshape(1, d))
```

```python
import functools

import jax
import jax.numpy as jnp
from jax import lax
from jax.experimental import pallas as pl
from jax.experimental.pallas import tpu as pltpu

F32 = jnp.float32
BF16 = jnp.bfloat16

POOL_WINDOWS = (2, 4, 8, 16)
POOL_HALO = 16
HEAD = 64
PAIR = 2 * HEAD
CHUNK = 64
INV_BASE = 8
N_SHIFT_MIX = 6
N_ADA = 6
NORM_EPS = 1e-6
LNX_EPS = 64e-5
SUBLANES = 8
VMEM_LIMIT = 60000 * 1024
ROWS_POOL = 256
ROWS_NORM = 256
ROWS_MIX = 128
ROWS_FFN = 1024
COLS_FFN = 512
COLS_FFN_UP = 256
COLS_FFN_OUT = 512
ROWS_LORA = 256
TILE_DENSE = 1024
COLS_ADA = 512
LANES_WKV = 16 * PAIR


def _params(*sem):
    return pltpu.CompilerParams(dimension_semantics=sem, vmem_limit_bytes=VMEM_LIMIT)


def _tile(n, want):
    t = min(n, want)
    assert n % t == 0, (n, want)
    return t


def _ada_kernel(c_ref, w_ref, b_ref, o_ref):
    c = c_ref[...]
    ca = (c * jax.nn.sigmoid(c)).astype(BF16)
    o_ref[...] = jnp.dot(ca, w_ref[...].astype(BF16), preferred_element_type=F32) + b_ref[...]


def _ada(c, ada_w, ada_b):
    depth, d, n = ada_w.shape
    b = c.shape[0]
    cp = jnp.zeros((SUBLANES, d), F32).at[:b].set(c)
    tn = _tile(n, COLS_ADA)
    out = pl.pallas_call(
        _ada_kernel,
        grid=(depth, n // tn),
        in_specs=[
            pl.BlockSpec((SUBLANES, d), lambda l, j: (0, 0)),
            pl.BlockSpec((None, d, tn), lambda l, j: (l, 0, j)),
            pl.BlockSpec((None, 1, tn), lambda l, j: (l, 0, j)),
        ],
        out_specs=pl.BlockSpec((None, SUBLANES, tn), lambda l, j: (l, 0, j)),
        out_shape=jax.ShapeDtypeStruct((depth, SUBLANES, n), F32),
        compiler_params=_params("arbitrary", "arbitrary"),
        name="ada_mod",
    )(cp, ada_w, ada_b.reshape(depth, 1, n))
    return out[:, :b]


def _rms_mod(x, g, sc, sh):
    ms = jnp.mean(x * x, axis=-1, keepdims=True)
    return (x * lax.rsqrt(ms + NORM_EPS)) * g * (1.0 + sc) + sh


def _row_spec(d):
    return pl.BlockSpec((None, 1, d), lambda b, t: (b, 0, 0))


def _par_spec(d):
    return pl.BlockSpec((1, d), lambda b, t: (0, 0))


def _pool_kernel(x_ref, g1_ref, sc1_ref, sh1_ref, gt1_ref, pw_ref, ps_ref,
                 g2_ref, sc2_ref, sh2_ref, x1_ref, h2_ref, carry_ref, *, tt, group):
    t = pl.program_id(1)

    @pl.when(t == 0)
    def _():
        carry_ref[...] = jnp.zeros_like(carry_ref)

    x = x_ref[...]
    h = _rms_mod(x, g1_ref[...], sc1_ref[...], sh1_ref[...])
    pos = (t * tt + lax.broadcasted_iota(jnp.int32, (tt, 1), 0) + 1).astype(F32)
    ys = []
    for gi, win in enumerate(POOL_WINDOWS):
        sl = slice(gi * group, (gi + 1) * group)
        hg = h[:, sl]
        s = jnp.concatenate([carry_ref[:, sl], hg], axis=0)
        shift = 1
        while shift < win:
            s = s + pltpu.roll(s, shift, axis=0)
            shift *= 2
        pooled = s[POOL_HALO:, :] / jnp.minimum(pos, float(win)) - hg
        ys.append(jnp.dot(pooled.astype(BF16), pw_ref[gi], preferred_element_type=F32))
    carry_ref[...] = h[tt - POOL_HALO:, :]
    y = jnp.concatenate(ys, axis=1) * ps_ref[...]
    x1 = x + gt1_ref[...] * y
    x1_ref[...] = x1
    h2_ref[...] = _rms_mod(x1, g2_ref[...], sc2_ref[...], sh2_ref[...]).astype(BF16)


def _pool_layer(x, g1, sc1, sh1, gt1, pool_w, pool_scale, g2, sc2, sh2):
    b, t, d = x.shape
    group = d // len(POOL_WINDOWS)
    tt = _tile(t, ROWS_POOL)
    tile = pl.BlockSpec((None, tt, d), lambda bi, ti: (bi, ti, 0))
    return pl.pallas_call(
        functools.partial(_pool_kernel, tt=tt, group=group),
        grid=(b, t // tt),
        in_specs=[tile, _par_spec(d), _row_spec(d), _row_spec(d), _row_spec(d),
                  pl.BlockSpec(pool_w.shape, lambda bi, ti: (0, 0, 0)), _par_spec(d),
                  _par_spec(d), _row_spec(d), _row_spec(d)],
        out_specs=[tile, tile],
        out_shape=[jax.ShapeDtypeStruct((b, t, d), F32), jax.ShapeDtypeStruct((b, t, d), BF16)],
        scratch_shapes=[pltpu.VMEM((POOL_HALO, d), F32)],
        compiler_params=_params("arbitrary", "arbitrary"),
        name="pool_mixer",
    )(x, g1, sc1, sh1, gt1, pool_w, pool_scale, g2, sc2, sh2)


def _gelu(x):
    return 0.5 * x * (1.0 + lax.erf(x * (2.0 ** -0.5)))


def _ffn_kernel(h_ref, wg_ref, wv_ref, cw_ref, cb_ref, wd_ref, y_ref, carry_ref, act_ref,
                *, tm, tn, nf, tiles_per_seq):
    s = pl.program_id(0)

    @pl.when(s == 0)
    def _():
        act_ref[...] = jnp.zeros_like(act_ref)

    act_prev = act_ref[...]
    jd = jnp.maximum(s - 1, 0) % nf
    d = y_ref.shape[-1]
    for n in range(d // tn):
        cols = slice(n * tn, (n + 1) * tn)
        part = jnp.dot(act_prev, wd_ref[:, cols], preferred_element_type=F32)
        y_ref[:, cols] = part + jnp.where(jd == 0, 0.0, y_ref[:, cols])

    su = jnp.minimum(s, pl.num_programs(0) - 2)
    first = (su // nf) % tiles_per_seq == 0
    j = su % nf
    h = h_ref[...]
    tf = act_ref.shape[-1]
    tu = min(tf, COLS_FFN_UP)
    for n in range(tf // tu):
        cols = slice(n * tu, (n + 1) * tu)
        ug = jnp.dot(h, wg_ref[:, cols], preferred_element_type=F32)
        uv = jnp.dot(h, wv_ref[:, cols], preferred_element_type=F32)
        prev = jnp.where(first, 0.0, carry_ref[j, :, cols])
        carry_ref[j, :, cols] = ug[tm - SUBLANES:, :]
        ext = jnp.concatenate([prev, ug], axis=0)
        u1 = ext[SUBLANES - 1:SUBLANES - 1 + tm, :]
        u2 = ext[SUBLANES - 2:SUBLANES - 2 + tm, :]
        gate = (ug * cw_ref[2:3, cols] + u1 * cw_ref[1:2, cols] + u2 * cw_ref[0:1, cols] + cb_ref[:, cols])
        act_ref[:, cols] = (_gelu(gate) * uv).astype(BF16)


def _ffn(h, w_gate, w_val, conv_w, conv_b, w_down, layer, seq_len):
    m, d = h.shape
    f = w_down.shape[1]
    tm = _tile(seq_len, ROWS_FFN)
    tf = _tile(f, COLS_FFN)
    nf = f // tf
    last = (m // tm) * nf - 1

    def up(s):
        return jnp.minimum(s, last)

    def down(s):
        return jnp.maximum(s - 1, 0)

    return pl.pallas_call(
        functools.partial(_ffn_kernel, tm=tm, tn=_tile(d, COLS_FFN_OUT), nf=nf, tiles_per_seq=seq_len // tm),
        grid=(last + 2,),
        in_specs=[
            pl.BlockSpec((tm, d), lambda s: (up(s) // nf, 0), pipeline_mode=pl.Buffered(1)),
            pl.BlockSpec((None, d, tf), lambda s: (layer, 0, up(s) % nf)),
            pl.BlockSpec((None, d, tf), lambda s: (layer, 0, up(s) % nf)),
            pl.BlockSpec((None, conv_w.shape[1], tf), lambda s: (layer, 0, up(s) % nf)),
            pl.BlockSpec((None, 1, tf), lambda s: (layer, 0, up(s) % nf)),
            pl.BlockSpec((None, tf, d), lambda s: (layer, down(s) % nf, 0)),
        ],
        out_specs=pl.BlockSpec((tm, d), lambda s: (down(s) // nf, 0), pipeline_mode=pl.Buffered(1)),
        out_shape=jax.ShapeDtypeStruct((m, d), F32),
        scratch_shapes=[pltpu.VMEM((nf, SUBLANES, tf), F32), pltpu.VMEM((tm, tf), BF16)],
        compiler_params=_params("arbitrary"),
        name="conv_glu",
    )(h, w_gate, w_val, conv_w, conv_b, w_down)


def _resnorm_kernel(x_ref, y_ref, gt_ref, g_ref, sc_ref, sh_ref, xo_ref, h_ref):
    x = x_ref[...] + gt_ref[...] * y_ref[...]
    xo_ref[...] = x
    h_ref[...] = _rms_mod(x, g_ref[...], sc_ref[...], sh_ref[...]).astype(h_ref.dtype)


def _resnorm(x, y, gt, g, sc, sh):
    b, t, d = x.shape
    tt = _tile(t, ROWS_NORM)
    tile = pl.BlockSpec((None, tt, d), lambda bi, ti: (bi, ti, 0))
    return pl.pallas_call(
        _resnorm_kernel,
        grid=(b, t // tt),
        in_specs=[tile, tile, _row_spec(d), _par_spec(d), _row_spec(d), _row_spec(d)],
        out_specs=[tile, tile],
        out_shape=[jax.ShapeDtypeStruct((b, t, d), F32), jax.ShapeDtypeStruct((b, t, d), BF16)],
        compiler_params=_params("arbitrary", "arbitrary"),
        name="residual_norm",
    )(x, y, gt, g, sc, sh)


def _resnorm_mix_kernel(x_ref, y_ref, gt_ref, g_ref, sc_ref, sh_ref, mu_ref, xo_ref, mix_ref,
                        carry_ref, *, tt):
    t = pl.program_id(1)

    @pl.when(t == 0)
    def _():
        carry_ref[...] = jnp.zeros_like(carry_ref)

    x = x_ref[...] + gt_ref[...] * y_ref[...]
    xo_ref[...] = x
    h = _rms_mod(x, g_ref[...], sc_ref[...], sh_ref[...])
    first_row = lax.broadcasted_iota(jnp.int32, (tt, 1), 0) == 0
    prev = jnp.where(first_row, carry_ref[SUBLANES - 1:SUBLANES, :], pltpu.roll(h, 1, axis=0))
    carry_ref[...] = h[tt - SUBLANES:, :]
    dx = prev - h
    mu = mu_ref[...]
    for n in range(N_SHIFT_MIX):
        mix_ref[n] = (h + dx * mu[n:n + 1, :]).astype(BF16)


def _resnorm_mix(x, y, gt, g, sc, sh, mu):
    b, t, d = x.shape
    tt = _tile(t, ROWS_MIX)
    tile = pl.BlockSpec((None, tt, d), lambda bi, ti: (bi, ti, 0))
    return pl.pallas_call(
        functools.partial(_resnorm_mix_kernel, tt=tt),
        grid=(b, t // tt),
        in_specs=[tile, tile, _row_spec(d), _par_spec(d), _row_spec(d), _row_spec(d),
                  pl.BlockSpec(mu.shape, lambda bi, ti: (0, 0))],
        out_specs=[tile, pl.BlockSpec((N_SHIFT_MIX, None, tt, d), lambda bi, ti: (0, bi, ti, 0))],
        out_shape=[jax.ShapeDtypeStruct((b, t, d), F32),
                   jax.ShapeDtypeStruct((N_SHIFT_MIX, b, t, d), BF16)],
        scratch_shapes=[pltpu.VMEM((SUBLANES, d), F32)],
        compiler_params=_params("arbitrary", "arbitrary"),
        name="residual_norm_mix",
    )(x, y, gt, g, sc, sh, mu)


def _final_kernel(x_ref, y_ref, gt_ref, g_ref, o_ref):
    x = x_ref[...] + gt_ref[...] * y_ref[...]
    ms = jnp.mean(x * x, axis=-1, keepdims=True)
    o_ref[...] = (x * lax.rsqrt(ms + NORM_EPS)) * g_ref[...]


def _final(x, y, gt, g):
    b, t, d = x.shape
    tt = _tile(t, ROWS_NORM)
    tile = pl.BlockSpec((None, tt, d), lambda bi, ti: (bi, ti, 0))
    return pl.pallas_call(
        _final_kernel,
        grid=(b, t // tt),
        in_specs=[tile, tile, _row_spec(d), _par_spec(d)],
        out_specs=tile,
        out_shape=jax.ShapeDtypeStruct((b, t, d), F32),
        compiler_params=_params("arbitrary", "arbitrary"),
        name="final_norm",
    )(x, y, gt, g)


def _matmul_kernel(a_ref, w_ref, o_ref):
    o_ref[...] = jnp.dot(a_ref[...], w_ref[...], preferred_element_type=F32)


def _matmul(a, which, w):
    _, m, k = a.shape
    n = w.shape[1]
    tm = _tile(m, TILE_DENSE)
    tn = _tile(n, TILE_DENSE)
    return pl.pallas_call(
        _matmul_kernel,
        grid=(m // tm, n // tn),
        in_specs=[pl.BlockSpec((None, tm, k), lambda i, j: (which, i, 0)),
                  pl.BlockSpec((k, tn), lambda i, j: (0, j))],
        out_specs=pl.BlockSpec((tm, tn), lambda i, j: (i, j)),
        out_shape=jax.ShapeDtypeStruct((m, n), F32),
        compiler_params=_params("arbitrary", "arbitrary"),
        name="dense",
    )(a, w)


def _lora_kernel(x_ref, w1_ref, w2_ref, o_ref, *, act):
    t = jnp.dot(x_ref[...], w1_ref[...], preferred_element_type=F32)
    if act == "tanh":
        t = jnp.tanh(t)
    elif act == "sigmoid":
        t = jax.nn.sigmoid(t)
    o_ref[...] = jnp.dot(t.astype(BF16), w2_ref[...], preferred_element_type=F32)


def _lora(mix, which, w1, w2, act):
    _, m, d = mix.shape
    r = w1.shape[1]
    tm = _tile(m, ROWS_LORA)
    return pl.pallas_call(
        functools.partial(_lora_kernel, act=act),
        grid=(m // tm,),
        in_specs=[pl.BlockSpec((None, tm, d), lambda i: (which, i, 0)),
                  pl.BlockSpec((d, r), lambda i: (0, 0)),
                  pl.BlockSpec((r, d), lambda i: (0, 0))],
        out_specs=pl.BlockSpec((tm, d), lambda i: (i, 0)),
        out_shape=jax.ShapeDtypeStruct((m, d), F32),
        compiler_params=_params("arbitrary"),
        name="lora_" + act,
    )(mix, w1, w2)


def _split(x):
    hi = x.astype(BF16)
    lo = (x - hi.astype(F32)).astype(BF16)
    return hi, lo


def _dg(a, b, ca, cb):
    return lax.dot_general(a, b, (((ca,), (cb,)), ((), ())), preferred_element_type=F32)


def _mm3(a, b, ca, cb):
    a_hi, a_lo = _split(a)
    b_hi, b_lo = _split(b)
    return _dg(jnp.concatenate([a_hi, a_hi, a_lo], axis=ca),
               jnp.concatenate([b_hi, b_lo, b_hi], axis=cb), ca, cb)


def _mm1(a, b, ca, cb):
    return _dg(a.astype(BF16), b.astype(BF16), ca, cb)


_mm = _mm1
_mm_state = _mm1


def _wkv_chunk(rs, ks, vs, wls, als, gs, prms, s0s, c):
    L = rs[0].shape[0]
    P = range(len(rs))
    w0, a0, kkp, kap, rkp, lng, lnb = [[prm[i:i + 1, :] for prm in prms] for i in range(7)]

    def stack(x):
        return jnp.concatenate([jnp.where(c["head0"], x, 0.0), jnp.where(c["head0"], 0.0, x)], axis=0)

    def headsum(x):
        hi, lo = _split(x)
        return _dg(jnp.concatenate([hi, lo], axis=1), c["ones2"], 1, 0)

    def cumsum(x):
        hi, lo = _split(x)
        return _dg(c["tril2"], jnp.concatenate([hi, lo], axis=0), 1, 0)

    def softplus(z):
        return jnp.maximum(z, 0.0) + jnp.log1p(jnp.exp(-jnp.abs(z)))

    logd = [-jnp.exp(-softplus(-(w0[p] + wls[p])) - 0.5) for p in P]
    lr = [jax.nn.sigmoid(a0[p] + als[p]) for p in P]
    kkr = [ks[p] * kkp[p] for p in P]
    k2 = [ks[p] * (1.0 + (lr[p] - 1.0) * kap[p]) for p in P]
    hs = [headsum(jnp.concatenate([kkr[p] * kkr[p], rs[p] * k2[p] * rkp[p]], axis=0)) for p in P]
    cum = [cumsum(logd[p]) for p in P]
    kk = [kkr[p] / jnp.maximum(jnp.sqrt(hs[p][:L]), 1e-12) for p in P]
    bonus = [hs[p][L:] for p in P]
    p_t = [jnp.exp(cum[p]) for p in P]
    inv_p = [jnp.exp(-cum[p]) for p in P]
    rt = [rs[p] * p_t[p] for p in P]
    at = [-kk[p] * jnp.exp(cum[p] - logd[p]) for p in P]
    bt = [kk[p] * lr[p] * inv_p[p] for p in P]
    kt = [k2[p] * inv_p[p] for p in P]

    bks = [jnp.concatenate([stack(bt[p]), stack(kt[p])], axis=0) for p in P]
    aa = [_mm(at[p], bks[p], 1, 1) for p in P]
    mm = [_mm1(rt[p], bks[p], 1, 1) for p in P]
    n = [jnp.where(c["strict"], aa[p][:, :PAIR], 0.0) for p in P]
    aak = [jnp.where(c["strict"], aa[p][:, PAIR:], 0.0) for p in P]
    mrbk = [jnp.concatenate([jnp.where(c["incl"], mm[p][:, :PAIR], 0.0),
                             jnp.where(c["incl"], mm[p][:, PAIR:], 0.0)], axis=1) for p in P]
    akv = [_mm(aak[p], stack(vs[p]), 1, 0) for p in P]

    npow = [jnp.where(c["diag"], n[p], 0.0) for p in P]
    tinv = [c["eye"] + npow[p] for p in P]
    span = 2
    while span < INV_BASE:
        npow = [_mm(npow[p], stack(npow[p]), 1, 0) for p in P]
        tinv = [tinv[p] + _mm(tinv[p], stack(npow[p]), 1, 0) for p in P]
        span *= 2
    for sub in c["sub"]:
        low = [_mm(jnp.where(sub, n[p], 0.0), stack(tinv[p]), 1, 0) for p in P]
        tinv = [tinv[p] + _mm(tinv[p], stack(low[p]), 1, 0) for p in P]

    tu = [_mm(tinv[p], jnp.concatenate([stack(at[p]), stack(akv[p])], axis=1), 1, 0) for p in P]
    u = [_mm_state(tu[p][:, :PAIR], s0s[p], 1, 1) + tu[p][:, PAIR:] for p in P]
    p_last = [p_t[p][L - 1:L, :] for p in P]
    s_new = [jnp.where(c["blockdiag"],
                       s0s[p] * p_last[p] + _mm_state(jnp.concatenate([u[p], vs[p]], axis=0),
                                                 jnp.concatenate([bt[p], kt[p]], axis=0) * p_last[p], 0, 0),
                       0.0) for p in P]
    y = [_mm1(rt[p], s0s[p], 1, 1)
         + _mm1(mrbk[p], jnp.concatenate([stack(u[p]), stack(vs[p])], axis=0), 1, 0) for p in P]

    mean = [headsum(y[p]) * (1.0 / HEAD) for p in P]
    dlt = [y[p] - mean[p] for p in P]
    var = [headsum(dlt[p] * dlt[p]) * (1.0 / HEAD) for p in P]
    outs = [((dlt[p] * lax.rsqrt(var[p] + LNX_EPS) * lng[p] + lnb[p]) + bonus[p] * vs[p]) * gs[p] for p in P]
    return outs, s_new


def _wkv_consts(L):
    lane = lax.broadcasted_iota(jnp.int32, (L, PAIR), 1)
    row = lax.broadcasted_iota(jnp.int32, (L, PAIR), 0)
    src = lane % HEAD
    r2 = lax.broadcasted_iota(jnp.int32, (PAIR, PAIR), 0) // HEAD
    c2 = lax.broadcasted_iota(jnp.int32, (PAIR, PAIR), 1) // HEAD
    blockdiag = r2 == c2
    ones = jnp.where(blockdiag, 1.0, 0.0).astype(BF16)
    tr = lax.broadcasted_iota(jnp.int32, (L, L), 0)
    tc = lax.broadcasted_iota(jnp.int32, (L, L), 1)
    tril = jnp.where(tr >= tc, 1.0, 0.0).astype(BF16)
    sub = []
    q = INV_BASE
    while q < L:
        sub.append((row // (2 * q) == src // (2 * q)) & (row // q == src // q + 1))
        q *= 2
    return {
        "head0": lane < HEAD,
        "strict": row > src,
        "incl": row >= src,
        "eye": jnp.where(row == src, 1.0, 0.0),
        "diag": row // INV_BASE == src // INV_BASE,
        "sub": sub,
        "blockdiag": blockdiag,
        "ones2": jnp.concatenate([ones, ones], axis=0),
        "tril2": jnp.concatenate([tril, tril], axis=1),
    }


def _wkv_kernel(r_ref, k_ref, v_ref, wl_ref, al_ref, g_ref, prm_ref, o_ref, s_ref, *, pairs):
    @pl.when(pl.program_id(2) == 0)
    def _():
        s_ref[...] = jnp.zeros_like(s_ref)

    c = _wkv_consts(r_ref.shape[0])
    cols = [slice(p * PAIR, (p + 1) * PAIR) for p in range(pairs)]
    outs, s_new = _wkv_chunk(*[[ref[:, cs] for cs in cols]
                               for ref in (r_ref, k_ref, v_ref, wl_ref, al_ref, g_ref, prm_ref)],
                             [s_ref[p] for p in range(pairs)], c)
    for p in range(pairs):
        s_ref[p] = s_new[p]
        o_ref[:, cols[p]] = outs[p].astype(o_ref.dtype)


def _wkv(r, k, v, wl, al, g, prm, batch, seq_len):
    m, d = r.shape
    L = _tile(seq_len, CHUNK)
    lanes = _tile(d, LANES_WKV)
    nchunk = seq_len // L
    tok = pl.BlockSpec((L, lanes), lambda b, hg, ci: (b * nchunk + ci, hg))
    return pl.pallas_call(
        functools.partial(_wkv_kernel, pairs=lanes // PAIR),
        grid=(batch, d // lanes, nchunk),
        in_specs=[tok] * 6 + [pl.BlockSpec((SUBLANES, lanes), lambda b, hg, ci: (0, hg))],
        out_specs=tok,
        out_shape=jax.ShapeDtypeStruct((m, d), BF16),
        scratch_shapes=[pltpu.VMEM((lanes // PAIR, PAIR, PAIR), F32)],
        compiler_params=_params("arbitrary", "arbitrary", "arbitrary"),
        name="wkv7_chunk",
    )(r, k, v, wl, al, g, prm)


def _pad_lora(w1, w2):
    r = w1.shape[1]
    rp = -(-r // 128) * 128
    return (jnp.pad(w1, ((0, 0), (0, rp - r))).astype(BF16),
            jnp.pad(w2, ((0, rp - r), (0, 0))).astype(BF16))


def _cast_pad_kernel(x_ref, o_ref, *, nblocks):
    keep = pl.program_id(1) < nblocks
    o_ref[...] = jnp.where(keep, x_ref[...], 0.0).astype(o_ref.dtype)


def _cast_pad(w, axis, start, size, out_size):
    depth = w.shape[0]
    tb = next(t for t in (512, 256, 128) if start % t == 0 and size % t == 0 and out_size % t == 0)
    nblocks, first = size // tb, start // tb
    block = (None, tb, w.shape[2]) if axis == 1 else (None, w.shape[1], tb)

    def src(l, j):
        jj = first + jnp.minimum(j, nblocks - 1)
        return (l, jj, 0) if axis == 1 else (l, 0, jj)

    def dst(l, j):
        return (l, j, 0) if axis == 1 else (l, 0, j)

    out_shape = (depth, out_size, w.shape[2]) if axis == 1 else (depth, w.shape[1], out_size)
    return pl.pallas_call(
        functools.partial(_cast_pad_kernel, nblocks=nblocks),
        grid=(depth, out_size // tb),
        in_specs=[pl.BlockSpec(block, src)],
        out_specs=pl.BlockSpec(block, dst),
        out_shape=jax.ShapeDtypeStruct(out_shape, BF16),
        compiler_params=_params("arbitrary", "arbitrary"),
        name="cast_pad",
    )(w)


def kernel(x, c, ada_w, ada_b, norm_g, pool_w, pool_scale, rwkv_mu, rwkv_wr, rwkv_wk, rwkv_wv,
           rwkv_w0, rwkv_w1, rwkv_w2, rwkv_a0, rwkv_a1, rwkv_a2, rwkv_g1, rwkv_g2, rwkv_kk,
           rwkv_ka, rwkv_rk, rwkv_lnx_g, rwkv_lnx_b, rwkv_wo, ffn_w_up, ffn_conv_w, ffn_conv_b,
           ffn_w_down, final_g):
    b, t, d = x.shape
    m = b * t
    depth = ada_w.shape[0]
    assert depth == 2, "layer 0 is the pooling mixer, layer 1 the RWKV-7 mixer"

    mod = _ada(c, ada_w, ada_b).reshape(depth, b, N_ADA, 1, d)
    sh1, sc1, gt1, sh2, sc2, gt2 = [[mod[l, :, n] for l in range(depth)] for n in range(N_ADA)]
    ng = norm_g.reshape(depth, 2, 1, d)

    f = ffn_w_down.shape[1]
    fpad = -f % min(f, COLS_FFN)
    w_gate = _cast_pad(ffn_w_up, 2, 0, f, f + fpad)
    w_val = _cast_pad(ffn_w_up, 2, f, f, f + fpad)
    w_down = _cast_pad(ffn_w_down, 1, 0, f, f + fpad)
    conv_w = jnp.pad(ffn_conv_w, ((0, 0), (0, 0), (0, fpad)))
    conv_b = jnp.pad(ffn_conv_b, ((0, 0), (0, fpad))).reshape(depth, 1, f + fpad)

    def ffn(h, l):
        return _ffn(h.reshape(m, d), w_gate, w_val, conv_w, conv_b, w_down, l, t).reshape(b, t, d)

    x1, h = _pool_layer(x, ng[0, 0], sc1[0], sh1[0], gt1[0], pool_w[0].astype(BF16),
                        pool_scale[0].reshape(1, d), ng[0, 1], sc2[0], sh2[0])
    y = ffn(h, 0)

    mu = rwkv_mu[0][jnp.array([0, 2, 3, 1, 4, 5])]
    x2, mix = _resnorm_mix(x1, y, gt2[0], ng[1, 0], sc1[1], sh1[1], mu)
    mix = mix.reshape(N_SHIFT_MIX, m, d)
    r, k, v = [_matmul(mix, n, w[0].astype(BF16)) for n, w in enumerate((rwkv_wr, rwkv_wk, rwkv_wv))]
    wl = _lora(mix, 3, *_pad_lora(rwkv_w1[0], rwkv_w2[0]), "tanh")
    al = _lora(mix, 4, *_pad_lora(rwkv_a1[0], rwkv_a2[0]), "none")
    gg = _lora(mix, 5, *_pad_lora(rwkv_g1[0], rwkv_g2[0]), "sigmoid")
    prm = jnp.stack([rwkv_w0[0], rwkv_a0[0], rwkv_kk[0], rwkv_ka[0], rwkv_rk[0].reshape(d),
                     rwkv_lnx_g[0], rwkv_lnx_b[0], jnp.zeros((d,), F32)])
    z = _wkv(r, k, v, wl, al, gg, prm, b, t)
    y = _matmul(z[None], 0, rwkv_wo[0].astype(BF16)).reshape(b, t, d)
    x3, h = _resnorm(x2, y, gt1[1], ng[1, 1], sc2[1], sh2[1])
    y = ffn(h, 1)
    return _final(x3, y, gt2[1], final_g.reshape(1, d))
```

```python
import functools

import jax
import jax.numpy as jnp
from jax import lax
from jax.experimental import pallas as pl
from jax.experimental.pallas import tpu as pltpu

F32 = jnp.float32
BF16 = jnp.bfloat16

POOL_WINDOWS = (2, 4, 8, 16)
POOL_HALO = 16
HEAD = 64
PAIR = 2 * HEAD
CHUNK = 64
INV_BASE = 8
N_SHIFT_MIX = 6
N_ADA = 6
NORM_EPS = 1e-6
LNX_EPS = 64e-5
SUBLANES = 8
VMEM_LIMIT = 60000 * 1024
ROWS_POOL = 256
ROWS_NORM = 256
ROWS_MIX = 128
ROWS_FFN = 1024
COLS_FFN = 512
COLS_FFN_UP = 256
COLS_FFN_OUT = 512
ROWS_LORA = 256
TILE_DENSE = 1024
COLS_ADA = 512
LANES_WKV = 16 * PAIR


def _params(*sem):
    return pltpu.CompilerParams(dimension_semantics=sem, vmem_limit_bytes=VMEM_LIMIT)


def _tile(n, want):
    t = min(n, want)
    assert n % t == 0, (n, want)
    return t


def _ada_kernel(c_ref, w_ref, b_ref, o_ref):
    c = c_ref[...]
    ca = (c * jax.nn.sigmoid(c)).astype(BF16)
    o_ref[...] = jnp.dot(ca, w_ref[...].astype(BF16), preferred_element_type=F32) + b_ref[...]


def _ada(c, ada_w, ada_b):
    depth, d, n = ada_w.shape
    b = c.shape[0]
    cp = jnp.zeros((SUBLANES, d), F32).at[:b].set(c)
    tn = _tile(n, COLS_ADA)
    out = pl.pallas_call(
        _ada_kernel,
        grid=(depth, n // tn),
        in_specs=[
            pl.BlockSpec((SUBLANES, d), lambda l, j: (0, 0)),
            pl.BlockSpec((None, d, tn), lambda l, j: (l, 0, j)),
            pl.BlockSpec((None, 1, tn), lambda l, j: (l, 0, j)),
        ],
        out_specs=pl.BlockSpec((None, SUBLANES, tn), lambda l, j: (l, 0, j)),
        out_shape=jax.ShapeDtypeStruct((depth, SUBLANES, n), F32),
        compiler_params=_params("arbitrary", "arbitrary"),
        name="ada_mod",
    )(cp, ada_w, ada_b.reshape(depth, 1, n))
    return out[:, :b]


def _rms_mod(x, g, sc, sh):
    ms = jnp.mean(x * x, axis=-1, keepdims=True)
    return (x * lax.rsqrt(ms + NORM_EPS)) * g * (1.0 + sc) + sh


def _row_spec(d):
    return pl.BlockSpec((None, 1, d), lambda b, t: (b, 0, 0))


def _par_spec(d):
    return pl.BlockSpec((1, d), lambda b, t: (0, 0))


def _pool_kernel(x_ref, g1_ref, sc1_ref, sh1_ref, gt1_ref, pw_ref, ps_ref,
                 g2_ref, sc2_ref, sh2_ref, x1_ref, h2_ref, carry_ref, *, tt, group):
    t = pl.program_id(1)

    @pl.when(t == 0)
    def _():
        carry_ref[...] = jnp.zeros_like(carry_ref)

    x = x_ref[...]
    h = _rms_mod(x, g1_ref[...], sc1_ref[...], sh1_ref[...])
    pos = (t * tt + lax.broadcasted_iota(jnp.int32, (tt, 1), 0) + 1).astype(F32)
    ys = []
    for gi, win in enumerate(POOL_WINDOWS):
        sl = slice(gi * group, (gi + 1) * group)
        hg = h[:, sl]
        s = jnp.concatenate([carry_ref[:, sl], hg], axis=0)
        shift = 1
        while shift < win:
            s = s + pltpu.roll(s, shift, axis=0)
            shift *= 2
        pooled = s[POOL_HALO:, :] / jnp.minimum(pos, float(win)) - hg
        ys.append(jnp.dot(pooled.astype(BF16), pw_ref[gi], preferred_element_type=F32))
    carry_ref[...] = h[tt - POOL_HALO:, :]
    y = jnp.concatenate(ys, axis=1) * ps_ref[...]
    x1 = x + gt1_ref[...] * y
    x1_ref[...] = x1
    h2_ref[...] = _rms_mod(x1, g2_ref[...], sc2_ref[...], sh2_ref[...]).astype(BF16)


def _pool_layer(x, g1, sc1, sh1, gt1, pool_w, pool_scale, g2, sc2, sh2):
    b, t, d = x.shape
    group = d // len(POOL_WINDOWS)
    tt = _tile(t, ROWS_POOL)
    tile = pl.BlockSpec((None, tt, d), lambda bi, ti: (bi, ti, 0))
    return pl.pallas_call(
        functools.partial(_pool_kernel, tt=tt, group=group),
        grid=(b, t // tt),
        in_specs=[tile, _par_spec(d), _row_spec(d), _row_spec(d), _row_spec(d),
                  pl.BlockSpec(pool_w.shape, lambda bi, ti: (0, 0, 0)), _par_spec(d),
                  _par_spec(d), _row_spec(d), _row_spec(d)],
        out_specs=[tile, tile],
        out_shape=[jax.ShapeDtypeStruct((b, t, d), F32), jax.ShapeDtypeStruct((b, t, d), BF16)],
        scratch_shapes=[pltpu.VMEM((POOL_HALO, d), F32)],
        compiler_params=_params("arbitrary", "arbitrary"),
        name="pool_mixer",
    )(x, g1, sc1, sh1, gt1, pool_w, pool_scale, g2, sc2, sh2)


def _gelu(x):
    return 0.5 * x * (1.0 + lax.erf(x * (2.0 ** -0.5)))


def _ffn_kernel(h_ref, wg_ref, wv_ref, cw_ref, cb_ref, wd_ref, y_ref, carry_ref, act_ref,
                *, tm, tn, nf, tiles_per_seq):
    s = pl.program_id(0)

    @pl.when(s == 0)
    def _():
        act_ref[...] = jnp.zeros_like(act_ref)

    act_prev = act_ref[...]
    jd = jnp.maximum(s - 1, 0) % nf
    d = y_ref.shape[-1]
    for n in range(d // tn):
        cols = slice(n * tn, (n + 1) * tn)
        part = jnp.dot(act_prev, wd_ref[:, cols], preferred_element_type=F32)
        y_ref[:, cols] = part + jnp.where(jd == 0, 0.0, y_ref[:, cols])

    su = jnp.minimum(s, pl.num_programs(0) - 2)
    first = (su // nf) % tiles_per_seq == 0
    j = su % nf
    h = h_ref[...]
    tf = act_ref.shape[-1]
    tu = min(tf, COLS_FFN_UP)
    for n in range(tf // tu):
        cols = slice(n * tu, (n + 1) * tu)
        ug = jnp.dot(h, wg_ref[:, cols], preferred_element_type=F32)
        uv = jnp.dot(h, wv_ref[:, cols], preferred_element_type=F32)
        prev = jnp.where(first, 0.0, carry_ref[j, :, cols])
        carry_ref[j, :, cols] = ug[tm - SUBLANES:, :]
        ext = jnp.concatenate([prev, ug], axis=0)
        u1 = ext[SUBLANES - 1:SUBLANES - 1 + tm, :]
        u2 = ext[SUBLANES - 2:SUBLANES - 2 + tm, :]
        gate = (ug * cw_ref[2:3, cols] + u1 * cw_ref[1:2, cols] + u2 * cw_ref[0:1, cols] + cb_ref[:, cols])
        act_ref[:, cols] = (_gelu(gate) * uv).astype(BF16)


def _ffn(h, w_gate, w_val, conv_w, conv_b, w_down, layer, seq_len):
    m, d = h.shape
    f = w_down.shape[1]
    tm = _tile(seq_len, ROWS_FFN)
    tf = _tile(f, COLS_FFN)
    nf = f // tf
    last = (m // tm) * nf - 1

    def up(s):
        return jnp.minimum(s, last)

    def down(s):
        return jnp.maximum(s - 1, 0)

    return pl.pallas_call(
        functools.partial(_ffn_kernel, tm=tm, tn=_tile(d, COLS_FFN_OUT), nf=nf, tiles_per_seq=seq_len // tm),
        grid=(last + 2,),
        in_specs=[
            pl.BlockSpec((tm, d), lambda s: (up(s) // nf, 0), pipeline_mode=pl.Buffered(1)),
            pl.BlockSpec((None, d, tf), lambda s: (layer, 0, up(s) % nf)),
            pl.BlockSpec((None, d, tf), lambda s: (layer, 0, up(s) % nf)),
            pl.BlockSpec((None, conv_w.shape[1], tf), lambda s: (layer, 0, up(s) % nf)),
            pl.BlockSpec((None, 1, tf), lambda s: (layer, 0, up(s) % nf)),
            pl.BlockSpec((None, tf, d), lambda s: (layer, down(s) % nf, 0)),
        ],
        out_specs=pl.BlockSpec((tm, d), lambda s: (down(s) // nf, 0), pipeline_mode=pl.Buffered(1)),
        out_shape=jax.ShapeDtypeStruct((m, d), F32),
        scratch_shapes=[pltpu.VMEM((nf, SUBLANES, tf), F32), pltpu.VMEM((tm, tf), BF16)],
        compiler_params=_params("arbitrary"),
        name="conv_glu",
    )(h, w_gate, w_val, conv_w, conv_b, w_down)


def _resnorm_kernel(x_ref, y_ref, gt_ref, g_ref, sc_ref, sh_ref, xo_ref, h_ref):
    x = x_ref[...] + gt_ref[...] * y_ref[...]
    xo_ref[...] = x
    h_ref[...] = _rms_mod(x, g_ref[...], sc_ref[...], sh_ref[...]).astype(h_ref.dtype)


def _resnorm(x, y, gt, g, sc, sh):
    b, t, d = x.shape
    tt = _tile(t, ROWS_NORM)
    tile = pl.BlockSpec((None, tt, d), lambda bi, ti: (bi, ti, 0))
    return pl.pallas_call(
        _resnorm_kernel,
        grid=(b, t // tt),
        in_specs=[tile, tile, _row_spec(d), _par_spec(d), _row_spec(d), _row_spec(d)],
        out_specs=[tile, tile],
        out_shape=[jax.ShapeDtypeStruct((b, t, d), F32), jax.ShapeDtypeStruct((b, t, d), BF16)],
        compiler_params=_params("arbitrary", "arbitrary"),
        name="residual_norm",
    )(x, y, gt, g, sc, sh)


def _resnorm_mix_kernel(x_ref, y_ref, gt_ref, g_ref, sc_ref, sh_ref, mu_ref, xo_ref, mix_ref,
                        carry_ref, *, tt):
    t = pl.program_id(1)

    @pl.when(t == 0)
    def _():
        carry_ref[...] = jnp.zeros_like(carry_ref)

    x = x_ref[...] + gt_ref[...] * y_ref[...]
    xo_ref[...] = x
    h = _rms_mod(x, g_ref[...], sc_ref[...], sh_ref[...])
    first_row = lax.broadcasted_iota(jnp.int32, (tt, 1), 0) == 0
    prev = jnp.where(first_row, carry_ref[SUBLANES - 1:SUBLANES, :], pltpu.roll(h, 1, axis=0))
    carry_ref[...] = h[tt - SUBLANES:, :]
    dx = prev - h
    mu = mu_ref[...]
    for n in range(N_SHIFT_MIX):
        mix_ref[n] = (h + dx * mu[n:n + 1, :]).astype(BF16)


def _resnorm_mix(x, y, gt, g, sc, sh, mu):
    b, t, d = x.shape
    tt = _tile(t, ROWS_MIX)
    tile = pl.BlockSpec((None, tt, d), lambda bi, ti: (bi, ti, 0))
    return pl.pallas_call(
        functools.partial(_resnorm_mix_kernel, tt=tt),
        grid=(b, t // tt),
        in_specs=[tile, tile, _row_spec(d), _par_spec(d), _row_spec(d), _row_spec(d),
                  pl.BlockSpec(mu.shape, lambda bi, ti: (0, 0))],
        out_specs=[tile, pl.BlockSpec((N_SHIFT_MIX, None, tt, d), lambda bi, ti: (0, bi, ti, 0))],
        out_shape=[jax.ShapeDtypeStruct((b, t, d), F32),
                   jax.ShapeDtypeStruct((N_SHIFT_MIX, b, t, d), BF16)],
        scratch_shapes=[pltpu.VMEM((SUBLANES, d), F32)],
        compiler_params=_params("arbitrary", "arbitrary"),
        name="residual_norm_mix",
    )(x, y, gt, g, sc, sh, mu)


def _final_kernel(x_ref, y_ref, gt_ref, g_ref, o_ref):
    x = x_ref[...] + gt_ref[...] * y_ref[...]
    ms = jnp.mean(x * x, axis=-1, keepdims=True)
    o_ref[...] = (x * lax.rsqrt(ms + NORM_EPS)) * g_ref[...]


def _final(x, y, gt, g):
    b, t, d = x.shape
    tt = _tile(t, ROWS_NORM)
    tile = pl.BlockSpec((None, tt, d), lambda bi, ti: (bi, ti, 0))
    return pl.pallas_call(
        _final_kernel,
        grid=(b, t // tt),
        in_specs=[tile, tile, _row_spec(d), _par_spec(d)],
        out_specs=tile,
        out_shape=jax.ShapeDtypeStruct((b, t, d), F32),
        compiler_params=_params("arbitrary", "arbitrary"),
        name="final_norm",
    )(x, y, gt, g)


def _matmul_kernel(a_ref, w_ref, o_ref):
    o_ref[...] = jnp.dot(a_ref[...], w_ref[...], preferred_element_type=F32)


def _matmul(a, which, w):
    _, m, k = a.shape
    n = w.shape[1]
    tm = _tile(m, TILE_DENSE)
    tn = _tile(n, TILE_DENSE)
    return pl.pallas_call(
        _matmul_kernel,
        grid=(m // tm, n // tn),
        in_specs=[pl.BlockSpec((None, tm, k), lambda i, j: (which, i, 0)),
                  pl.BlockSpec((k, tn), lambda i, j: (0, j))],
        out_specs=pl.BlockSpec((tm, tn), lambda i, j: (i, j)),
        out_shape=jax.ShapeDtypeStruct((m, n), F32),
        compiler_params=_params("arbitrary", "arbitrary"),
        name="dense",
    )(a, w)


def _lora_kernel(x_ref, w1_ref, w2_ref, o_ref, *, act):
    t = jnp.dot(x_ref[...], w1_ref[...], preferred_element_type=F32)
    if act == "tanh":
        t = jnp.tanh(t)
    elif act == "sigmoid":
        t = jax.nn.sigmoid(t)
    o_ref[...] = jnp.dot(t.astype(BF16), w2_ref[...], preferred_element_type=F32)


def _lora(mix, which, w1, w2, act):
    _, m, d = mix.shape
    r = w1.shape[1]
    tm = _tile(m, ROWS_LORA)
    return pl.pallas_call(
        functools.partial(_lora_kernel, act=act),
        grid=(m // tm,),
        in_specs=[pl.BlockSpec((None, tm, d), lambda i: (which, i, 0)),
                  pl.BlockSpec((d, r), lambda i: (0, 0)),
                  pl.BlockSpec((r, d), lambda i: (0, 0))],
        out_specs=pl.BlockSpec((tm, d), lambda i: (i, 0)),
        out_shape=jax.ShapeDtypeStruct((m, d), F32),
        compiler_params=_params("arbitrary"),
        name="lora_" + act,
    )(mix, w1, w2)


def _split(x):
    hi = x.astype(BF16)
    lo = (x - hi.astype(F32)).astype(BF16)
    return hi, lo


def _dg(a, b, ca, cb):
    return lax.dot_general(a, b, (((ca,), (cb,)), ((), ())), preferred_element_type=F32)


def _mm3(a, b, ca, cb):
    a_hi, a_lo = _split(a)
    b_hi, b_lo = _split(b)
    return _dg(jnp.concatenate([a_hi, a_hi, a_lo], axis=ca),
               jnp.concatenate([b_hi, b_lo, b_hi], axis=cb), ca, cb)


def _mm1(a, b, ca, cb):
    return _dg(a.astype(BF16), b.astype(BF16), ca, cb)


_mm = _mm1
_mm_state = _mm1


def _wkv_chunk(rs, ks, vs, wls, als, gs, prms, s0s, c):
    L = rs[0].shape[0]
    P = range(len(rs))
    w0, a0, kkp, kap, rkp, lng, lnb = [[prm[i:i + 1, :] for prm in prms] for i in range(7)]

    def stack(x):
        return jnp.concatenate([jnp.where(c["head0"], x, 0.0), jnp.where(c["head0"], 0.0, x)], axis=0)

    def headsum(xs):
        rows = xs[0].shape[0]
        hi, lo = _split(jnp.concatenate(xs, axis=0))
        sums = _dg(jnp.concatenate([hi, lo], axis=1), c["ones2"], 1, 0)
        return [sums[p * rows:(p + 1) * rows] for p in P]

    def cumsum(xs):
        out = []
        for p in range(0, len(xs), 2):
            hi, lo = _split(jnp.concatenate(xs[p:p + 2], axis=1))
            both = _dg(c["tril2"], jnp.concatenate([hi, lo], axis=0), 1, 0)
            out += [both[:, q * PAIR:(q + 1) * PAIR] for q in range(len(xs[p:p + 2]))]
        return out

    def softplus(z):
        return jnp.maximum(z, 0.0) + jnp.log1p(jnp.exp(-jnp.abs(z)))

    logd = [-jnp.exp(-softplus(-(w0[p] + wls[p])) - 0.5) for p in P]
    lr = [jax.nn.sigmoid(a0[p] + als[p]) for p in P]
    kkr = [ks[p] * kkp[p] for p in P]
    k2 = [ks[p] * (1.0 + (lr[p] - 1.0) * kap[p]) for p in P]
    hs = headsum([jnp.concatenate([kkr[p] * kkr[p], rs[p] * k2[p] * rkp[p]], axis=0) for p in P])
    cum = cumsum(logd)
    kk = [kkr[p] / jnp.maximum(jnp.sqrt(hs[p][:L]), 1e-12) for p in P]
    bonus = [hs[p][L:] for p in P]
    p_t = [jnp.exp(cum[p]) for p in P]
    inv_p = [jnp.exp(-cum[p]) for p in P]
    rt = [rs[p] * p_t[p] for p in P]
    at = [-kk[p] * jnp.exp(cum[p] - logd[p]) for p in P]
    bt = [kk[p] * lr[p] * inv_p[p] for p in P]
    kt = [k2[p] * inv_p[p] for p in P]

    bks = [jnp.concatenate([stack(bt[p]), stack(kt[p])], axis=0) for p in P]
    am = [_mm(jnp.concatenate([at[p], rt[p]], axis=0), bks[p], 1, 1) for p in P]
    n = [jnp.where(c["strict"], am[p][:L, :PAIR], 0.0) for p in P]
    aak = [jnp.where(c["strict"], am[p][:L, PAIR:], 0.0) for p in P]
    mrbk = [jnp.concatenate([jnp.where(c["incl"], am[p][L:, :PAIR], 0.0),
                             jnp.where(c["incl"], am[p][L:, PAIR:], 0.0)], axis=1) for p in P]
    akv = [_mm(aak[p], stack(vs[p]), 1, 0) for p in P]

    npow = [jnp.where(c["diag"], n[p], 0.0) for p in P]
    tinv = [c["eye"] + npow[p] for p in P]
    span = 2
    while span < INV_BASE:
        npow = [_mm(npow[p], stack(npow[p]), 1, 0) for p in P]
        tinv = [tinv[p] + _mm(tinv[p], stack(npow[p]), 1, 0) for p in P]
        span *= 2
    for sub in c["sub"]:
        low = [_mm(jnp.where(sub, n[p], 0.0), stack(tinv[p]), 1, 0) for p in P]
        tinv = [tinv[p] + _mm(tinv[p], stack(low[p]), 1, 0) for p in P]

    tu = [_mm(tinv[p], jnp.concatenate([stack(at[p]), stack(akv[p])], axis=1), 1, 0) for p in P]
    us = [_mm_state(jnp.concatenate([tu[p][:, :PAIR], rt[p]], axis=0), s0s[p], 1, 1) for p in P]
    u = [us[p][:L] + tu[p][:, PAIR:] for p in P]
    p_last = [p_t[p][L - 1:L, :] for p in P]
    s_new = [jnp.where(c["blockdiag"],
                       s0s[p] * p_last[p] + _mm_state(jnp.concatenate([u[p], vs[p]], axis=0),
                                                 jnp.concatenate([bt[p], kt[p]], axis=0) * p_last[p], 0, 0),
                       0.0) for p in P]
    y = [us[p][L:] + _mm1(mrbk[p], jnp.concatenate([stack(u[p]), stack(vs[p])], axis=0), 1, 0) for p in P]

    mean = headsum(y)
    dlt = [y[p] - mean[p] * (1.0 / HEAD) for p in P]
    var = headsum([dlt[p] * dlt[p] for p in P])
    outs = [((dlt[p] * lax.rsqrt(var[p] * (1.0 / HEAD) + LNX_EPS) * lng[p] + lnb[p]) + bonus[p] * vs[p]) * gs[p]
            for p in P]
    return outs, s_new


def _wkv_consts(L):
    lane = lax.broadcasted_iota(jnp.int32, (L, PAIR), 1)
    row = lax.broadcasted_iota(jnp.int32, (L, PAIR), 0)
    src = lane % HEAD
    r2 = lax.broadcasted_iota(jnp.int32, (PAIR, PAIR), 0) // HEAD
    c2 = lax.broadcasted_iota(jnp.int32, (PAIR, PAIR), 1) // HEAD
    blockdiag = r2 == c2
    ones = jnp.where(blockdiag, 1.0, 0.0).astype(BF16)
    tr = lax.broadcasted_iota(jnp.int32, (L, L), 0)
    tc = lax.broadcasted_iota(jnp.int32, (L, L), 1)
    tril = jnp.where(tr >= tc, 1.0, 0.0).astype(BF16)
    sub = []
    q = INV_BASE
    while q < L:
        sub.append((row // (2 * q) == src // (2 * q)) & (row // q == src // q + 1))
        q *= 2
    return {
        "head0": lane < HEAD,
        "strict": row > src,
        "incl": row >= src,
        "eye": jnp.where(row == src, 1.0, 0.0),
        "diag": row // INV_BASE == src // INV_BASE,
        "sub": sub,
        "blockdiag": blockdiag,
        "ones2": jnp.concatenate([ones, ones], axis=0),
        "tril2": jnp.concatenate([tril, tril], axis=1),
    }


def _wkv_kernel(r_ref, k_ref, v_ref, wl_ref, al_ref, g_ref, prm_ref, o_ref, s_ref, *, pairs):
    @pl.when(pl.program_id(2) == 0)
    def _():
        s_ref[...] = jnp.zeros_like(s_ref)

    c = _wkv_consts(r_ref.shape[0])
    cols = [slice(p * PAIR, (p + 1) * PAIR) for p in range(pairs)]
    outs, s_new = _wkv_chunk(*[[ref[:, cs] for cs in cols]
                               for ref in (r_ref, k_ref, v_ref, wl_ref, al_ref, g_ref, prm_ref)],
                             [s_ref[p] for p in range(pairs)], c)
    for p in range(pairs):
        s_ref[p] = s_new[p]
        o_ref[:, cols[p]] = outs[p].astype(o_ref.dtype)


def _wkv(r, k, v, wl, al, g, prm, batch, seq_len):
    m, d = r.shape
    L = _tile(seq_len, CHUNK)
    lanes = _tile(d, LANES_WKV)
    nchunk = seq_len // L
    tok = pl.BlockSpec((L, lanes), lambda b, hg, ci: (b * nchunk + ci, hg))
    return pl.pallas_call(
        functools.partial(_wkv_kernel, pairs=lanes // PAIR),
        grid=(batch, d // lanes, nchunk),
        in_specs=[tok] * 6 + [pl.BlockSpec((SUBLANES, lanes), lambda b, hg, ci: (0, hg))],
        out_specs=tok,
        out_shape=jax.ShapeDtypeStruct((m, d), BF16),
        scratch_shapes=[pltpu.VMEM((lanes // PAIR, PAIR, PAIR), F32)],
        compiler_params=_params("arbitrary", "arbitrary", "arbitrary"),
        name="wkv7_chunk",
    )(r, k, v, wl, al, g, prm)


def _pad_lora(w1, w2):
    r = w1.shape[1]
    rp = -(-r // 128) * 128
    return (jnp.pad(w1, ((0, 0), (0, rp - r))).astype(BF16),
            jnp.pad(w2, ((0, rp - r), (0, 0))).astype(BF16))


def _cast_pad_kernel(x_ref, o_ref, *, nblocks):
    keep = pl.program_id(1) < nblocks
    o_ref[...] = jnp.where(keep, x_ref[...], 0.0).astype(o_ref.dtype)


def _cast_pad(w, axis, start, size, out_size):
    depth = w.shape[0]
    tb = next(t for t in (512, 256, 128) if start % t == 0 and size % t == 0 and out_size % t == 0)
    nblocks, first = size // tb, start // tb
    block = (None, tb, w.shape[2]) if axis == 1 else (None, w.shape[1], tb)

    def src(l, j):
        jj = first + jnp.minimum(j, nblocks - 1)
        return (l, jj, 0) if axis == 1 else (l, 0, jj)

    def dst(l, j):
        return (l, j, 0) if axis == 1 else (l, 0, j)

    out_shape = (depth, out_size, w.shape[2]) if axis == 1 else (depth, w.shape[1], out_size)
    return pl.pallas_call(
        functools.partial(_cast_pad_kernel, nblocks=nblocks),
        grid=(depth, out_size // tb),
        in_specs=[pl.BlockSpec(block, src)],
        out_specs=pl.BlockSpec(block, dst),
        out_shape=jax.ShapeDtypeStruct(out_shape, BF16),
        compiler_params=_params("arbitrary", "arbitrary"),
        name="cast_pad",
    )(w)


def kernel(x, c, ada_w, ada_b, norm_g, pool_w, pool_scale, rwkv_mu, rwkv_wr, rwkv_wk, rwkv_wv,
           rwkv_w0, rwkv_w1, rwkv_w2, rwkv_a0, rwkv_a1, rwkv_a2, rwkv_g1, rwkv_g2, rwkv_kk,
           rwkv_ka, rwkv_rk, rwkv_lnx_g, rwkv_lnx_b, rwkv_wo, ffn_w_up, ffn_conv_w, ffn_conv_b,
           ffn_w_down, final_g):
    b, t, d = x.shape
    m = b * t
    depth = ada_w.shape[0]
    assert depth == 2, "layer 0 is the pooling mixer, layer 1 the RWKV-7 mixer"

    mod = _ada(c, ada_w, ada_b).reshape(depth, b, N_ADA, 1, d)
    sh1, sc1, gt1, sh2, sc2, gt2 = [[mod[l, :, n] for l in range(depth)] for n in range(N_ADA)]
    ng = norm_g.reshape(depth, 2, 1, d)

    f = ffn_w_down.shape[1]
    fpad = -f % min(f, COLS_FFN)
    w_gate = _cast_pad(ffn_w_up, 2, 0, f, f + fpad)
    w_val = _cast_pad(ffn_w_up, 2, f, f, f + fpad)
    w_down = _cast_pad(ffn_w_down, 1, 0, f, f + fpad)
    conv_w = jnp.pad(ffn_conv_w, ((0, 0), (0, 0), (0, fpad)))
    conv_b = jnp.pad(ffn_conv_b, ((0, 0), (0, fpad))).reshape(depth, 1, f + fpad)

    def ffn(h, l):
        return _ffn(h.reshape(m, d), w_gate, w_val, conv_w, conv_b, w_down, l, t).reshape(b, t, d)

    x1, h = _pool_layer(x, ng[0, 0], sc1[0], sh1[0], gt1[0], pool_w[0].astype(BF16),
                        pool_scale[0].reshape(1, d), ng[0, 1], sc2[0], sh2[0])
    y = ffn(h, 0)

    mu = rwkv_mu[0][jnp.array([0, 2, 3, 1, 4, 5])]
    x2, mix = _resnorm_mix(x1, y, gt2[0], ng[1, 0], sc1[1], sh1[1], mu)
    mix = mix.reshape(N_SHIFT_MIX, m, d)
    r, k, v = [_matmul(mix, n, w[0].astype(BF16)) for n, w in enumerate((rwkv_wr, rwkv_wk, rwkv_wv))]
    wl = _lora(mix, 3, *_pad_lora(rwkv_w1[0], rwkv_w2[0]), "tanh")
    al = _lora(mix, 4, *_pad_lora(rwkv_a1[0], rwkv_a2[0]), "none")
    gg = _lora(mix, 5, *_pad_lora(rwkv_g1[0], rwkv_g2[0]), "sigmoid")
    prm = jnp.stack([rwkv_w0[0], rwkv_a0[0], rwkv_kk[0], rwkv_ka[0], rwkv_rk[0].reshape(d),
                     rwkv_lnx_g[0], rwkv_lnx_b[0], jnp.zeros((d,), F32)])
    z = _wkv(r, k, v, wl, al, gg, prm, b, t)
    y = _matmul(z[None], 0, rwkv_wo[0].astype(BF16)).reshape(b, t, d)
    x3, h = _resnorm(x2, y, gt1[1], ng[1, 1], sc2[1], sh2[1])
    y = ffn(h, 1)
    return _final(x3, y, gt2[1], final_g.reshape(1, d))
```

```python
import functools

import jax
import jax.numpy as jnp
from jax import lax
from jax.experimental import pallas as pl
from jax.experimental.pallas import tpu as pltpu

F32 = jnp.float32
BF16 = jnp.bfloat16

POOL_WINDOWS = (2, 4, 8, 16)
POOL_HALO = 16
HEAD = 64
PAIR = 2 * HEAD
CHUNK = 64
INV_BASE = 8
N_SHIFT_MIX = 6
N_ADA = 6
NORM_EPS = 1e-6
LNX_EPS = 64e-5
SUBLANES = 8
LANES = 128
MIX_ROWS = 16
MIX_COLS = 512
VMEM_LIMIT = 60000 * 1024
ROWS_POOL = 256
ROWS_NORM = 256
ROWS_MIX = 128
ROWS_FFN = 1024
COLS_FFN = 512
COLS_FFN_UP = 256
COLS_FFN_OUT = 512
ROWS_LORA = 256
TILE_DENSE = 1024
COLS_ADA = 512
LANES_WKV = 16 * PAIR


def _params(*sem):
    return pltpu.CompilerParams(dimension_semantics=sem, vmem_limit_bytes=VMEM_LIMIT)


def _tile(n, want):
    t = min(n, want)
    assert n % t == 0, (n, want)
    return t


def _ada_kernel(c_ref, w_ref, b_ref, o_ref):
    c = c_ref[...]
    ca = (c * jax.nn.sigmoid(c)).astype(BF16)
    o_ref[...] = jnp.dot(ca, w_ref[...].astype(BF16), preferred_element_type=F32) + b_ref[...]


def _ada(c, ada_w, ada_b):
    depth, d, n = ada_w.shape
    b = c.shape[0]
    cp = jnp.zeros((SUBLANES, d), F32).at[:b].set(c)
    tn = _tile(n, COLS_ADA)
    out = pl.pallas_call(
        _ada_kernel,
        grid=(depth, n // tn),
        in_specs=[
            pl.BlockSpec((SUBLANES, d), lambda l, j: (0, 0)),
            pl.BlockSpec((None, d, tn), lambda l, j: (l, 0, j)),
            pl.BlockSpec((None, 1, tn), lambda l, j: (l, 0, j)),
        ],
        out_specs=pl.BlockSpec((None, SUBLANES, tn), lambda l, j: (l, 0, j)),
        out_shape=jax.ShapeDtypeStruct((depth, SUBLANES, n), F32),
        compiler_params=_params("arbitrary", "arbitrary"),
        name="ada_mod",
    )(cp, ada_w, ada_b.reshape(depth, 1, n))
    return out[:, :b]


def _rms_mod(x, g, sc, sh):
    ms = jnp.mean(x * x, axis=-1, keepdims=True)
    return (x * lax.rsqrt(ms + NORM_EPS)) * g * (1.0 + sc) + sh


def _row_spec(d):
    return pl.BlockSpec((None, 1, d), lambda b, t: (b, 0, 0))


def _par_spec(d):
    return pl.BlockSpec((1, d), lambda b, t: (0, 0))


def _pool_kernel(x_ref, g1_ref, sc1_ref, sh1_ref, gt1_ref, pw_ref, ps_ref,
                 g2_ref, sc2_ref, sh2_ref, x1_ref, h2_ref, carry_ref, *, tt, group):
    t = pl.program_id(1)

    @pl.when(t == 0)
    def _():
        carry_ref[...] = jnp.zeros_like(carry_ref)

    x = x_ref[...]
    h = _rms_mod(x, g1_ref[...], sc1_ref[...], sh1_ref[...])
    pos = (t * tt + lax.broadcasted_iota(jnp.int32, (tt, 1), 0) + 1).astype(F32)
    ys = []
    for gi, win in enumerate(POOL_WINDOWS):
        sl = slice(gi * group, (gi + 1) * group)
        hg = h[:, sl]
        s = jnp.concatenate([carry_ref[:, sl], hg], axis=0)
        shift = 1
        while shift < win:
            s = s + pltpu.roll(s, shift, axis=0)
            shift *= 2
        pooled = s[POOL_HALO:, :] / jnp.minimum(pos, float(win)) - hg
        ys.append(jnp.dot(pooled.astype(BF16), pw_ref[gi], preferred_element_type=F32))
    carry_ref[...] = h[tt - POOL_HALO:, :]
    y = jnp.concatenate(ys, axis=1) * ps_ref[...]
    x1 = x + gt1_ref[...] * y
    x1_ref[...] = x1
    h2_ref[...] = _rms_mod(x1, g2_ref[...], sc2_ref[...], sh2_ref[...]).astype(BF16)


def _pool_layer(x, g1, sc1, sh1, gt1, pool_w, pool_scale, g2, sc2, sh2):
    b, t, d = x.shape
    group = d // len(POOL_WINDOWS)
    tt = _tile(t, ROWS_POOL)
    tile = pl.BlockSpec((None, tt, d), lambda bi, ti: (bi, ti, 0))
    return pl.pallas_call(
        functools.partial(_pool_kernel, tt=tt, group=group),
        grid=(b, t // tt),
        in_specs=[tile, _par_spec(d), _row_spec(d), _row_spec(d), _row_spec(d),
                  pl.BlockSpec(pool_w.shape, lambda bi, ti: (0, 0, 0)), _par_spec(d),
                  _par_spec(d), _row_spec(d), _row_spec(d)],
        out_specs=[tile, tile],
        out_shape=[jax.ShapeDtypeStruct((b, t, d), F32), jax.ShapeDtypeStruct((b, t, d), BF16)],
        scratch_shapes=[pltpu.VMEM((POOL_HALO, d), F32)],
        compiler_params=_params("arbitrary", "arbitrary"),
        name="pool_mixer",
    )(x, g1, sc1, sh1, gt1, pool_w, pool_scale, g2, sc2, sh2)


def _gelu(x):
    return 0.5 * x * (1.0 + lax.erf(x * (2.0 ** -0.5)))


def _ffn_kernel(h_ref, wg_ref, wv_ref, cw_ref, cb_ref, wd_ref, y_ref, carry_ref, act_ref,
                *, tm, tn, nf, tiles_per_seq):
    s = pl.program_id(0)

    @pl.when(s == 0)
    def _():
        act_ref[...] = jnp.zeros_like(act_ref)

    act_prev = act_ref[...]
    jd = jnp.maximum(s - 1, 0) % nf
    d = y_ref.shape[-1]
    for n in range(d // tn):
        cols = slice(n * tn, (n + 1) * tn)
        part = jnp.dot(act_prev, wd_ref[:, cols], preferred_element_type=F32)
        y_ref[:, cols] = part + jnp.where(jd == 0, 0.0, y_ref[:, cols])

    su = jnp.minimum(s, pl.num_programs(0) - 2)
    first = (su // nf) % tiles_per_seq == 0
    j = su % nf
    h = h_ref[...]
    tf = act_ref.shape[-1]
    tu = min(tf, COLS_FFN_UP)
    for n in range(tf // tu):
        cols = slice(n * tu, (n + 1) * tu)
        ug = jnp.dot(h, wg_ref[:, cols], preferred_element_type=F32)
        uv = jnp.dot(h, wv_ref[:, cols], preferred_element_type=F32)
        prev = jnp.where(first, 0.0, carry_ref[j, :, cols])
        carry_ref[j, :, cols] = ug[tm - SUBLANES:, :]
        ext = jnp.concatenate([prev, ug], axis=0)
        u1 = ext[SUBLANES - 1:SUBLANES - 1 + tm, :]
        u2 = ext[SUBLANES - 2:SUBLANES - 2 + tm, :]
        gate = (ug * cw_ref[2:3, cols] + u1 * cw_ref[1:2, cols] + u2 * cw_ref[0:1, cols] + cb_ref[:, cols])
        act_ref[:, cols] = (_gelu(gate) * uv).astype(BF16)


def _ffn(h, w_gate, w_val, conv_w, conv_b, w_down, layer, seq_len):
    m, d = h.shape
    f = w_down.shape[1]
    tm = _tile(seq_len, ROWS_FFN)
    tf = _tile(f, COLS_FFN)
    nf = f // tf
    last = (m // tm) * nf - 1

    def up(s):
        return jnp.minimum(s, last)

    def down(s):
        return jnp.maximum(s - 1, 0)

    return pl.pallas_call(
        functools.partial(_ffn_kernel, tm=tm, tn=_tile(d, COLS_FFN_OUT), nf=nf, tiles_per_seq=seq_len // tm),
        grid=(last + 2,),
        in_specs=[
            pl.BlockSpec((tm, d), lambda s: (up(s) // nf, 0), pipeline_mode=pl.Buffered(1)),
            pl.BlockSpec((None, d, tf), lambda s: (layer, 0, up(s) % nf)),
            pl.BlockSpec((None, d, tf), lambda s: (layer, 0, up(s) % nf)),
            pl.BlockSpec((None, conv_w.shape[1], tf), lambda s: (layer, 0, up(s) % nf)),
            pl.BlockSpec((None, 1, tf), lambda s: (layer, 0, up(s) % nf)),
            pl.BlockSpec((None, tf, d), lambda s: (layer, down(s) % nf, 0)),
        ],
        out_specs=pl.BlockSpec((tm, d), lambda s: (down(s) // nf, 0), pipeline_mode=pl.Buffered(1)),
        out_shape=jax.ShapeDtypeStruct((m, d), F32),
        scratch_shapes=[pltpu.VMEM((nf, SUBLANES, tf), F32), pltpu.VMEM((tm, tf), BF16)],
        compiler_params=_params("arbitrary"),
        name="conv_glu",
    )(h, w_gate, w_val, conv_w, conv_b, w_down)


def _resnorm_kernel(x_ref, y_ref, gt_ref, g_ref, sc_ref, sh_ref, xo_ref, h_ref):
    x = x_ref[...] + gt_ref[...] * y_ref[...]
    xo_ref[...] = x
    h_ref[...] = _rms_mod(x, g_ref[...], sc_ref[...], sh_ref[...]).astype(h_ref.dtype)


def _resnorm(x, y, gt, g, sc, sh):
    b, t, d = x.shape
    tt = _tile(t, ROWS_NORM)
    tile = pl.BlockSpec((None, tt, d), lambda bi, ti: (bi, ti, 0))
    return pl.pallas_call(
        _resnorm_kernel,
        grid=(b, t // tt),
        in_specs=[tile, tile, _row_spec(d), _par_spec(d), _row_spec(d), _row_spec(d)],
        out_specs=[tile, tile],
        out_shape=[jax.ShapeDtypeStruct((b, t, d), F32), jax.ShapeDtypeStruct((b, t, d), BF16)],
        compiler_params=_params("arbitrary", "arbitrary"),
        name="residual_norm",
    )(x, y, gt, g, sc, sh)


def _resnorm_mix_kernel(x_ref, y_ref, gt_ref, g_ref, sc_ref, sh_ref, mu_ref, xo_ref, mix_ref,
                        carry_ref, *, tt):
    t = pl.program_id(1)

    @pl.when(t == 0)
    def _():
        carry_ref[...] = jnp.zeros_like(carry_ref)

    d = x_ref.shape[-1]
    cb = min(d, MIX_COLS)
    chunks = [slice(c0, c0 + cb) for c0 in range(0, d, cb)]
    first_row = lax.broadcasted_iota(jnp.int32, (MIX_ROWS, 1), 0) == 0
    last = SUBLANES - 1

    def row_block(r, carry):
        rows = pl.ds(pl.multiple_of(r * MIX_ROWS, MIX_ROWS), MIX_ROWS)
        ss = jnp.zeros((MIX_ROWS, LANES), F32)
        for cols in chunks:
            x = x_ref[rows, cols] + gt_ref[:, cols] * y_ref[rows, cols]
            xo_ref[rows, cols] = x
            xx = x * x
            for l0 in range(0, cb, LANES):
                ss = ss + xx[:, l0:l0 + LANES]
        inv = lax.rsqrt(jnp.sum(ss, axis=-1, keepdims=True) * (1.0 / d) + NORM_EPS)
        for cols in chunks:
            h = (xo_ref[rows, cols] * inv) * g_ref[:, cols] * (1.0 + sc_ref[:, cols]) + sh_ref[:, cols]
            prev = jnp.where(first_row, carry_ref[last:, cols], pltpu.roll(h, 1, axis=0))
            carry_ref[last:, cols] = h[MIX_ROWS - 1:, :]
            dx = prev - h
            for n in range(N_SHIFT_MIX):
                mix_ref[n, rows, cols] = (h + dx * mu_ref[n:n + 1, cols]).astype(BF16)
        return carry

    lax.fori_loop(0, tt // MIX_ROWS, row_block, 0)


def _resnorm_mix(x, y, gt, g, sc, sh, mu):
    b, t, d = x.shape
    tt = _tile(t, ROWS_MIX)
    tile = pl.BlockSpec((None, tt, d), lambda bi, ti: (bi, ti, 0))
    return pl.pallas_call(
        functools.partial(_resnorm_mix_kernel, tt=tt),
        grid=(b, t // tt),
        in_specs=[tile, tile, _row_spec(d), _par_spec(d), _row_spec(d), _row_spec(d),
                  pl.BlockSpec(mu.shape, lambda bi, ti: (0, 0))],
        out_specs=[tile, pl.BlockSpec((N_SHIFT_MIX, None, tt, d), lambda bi, ti: (0, bi, ti, 0))],
        out_shape=[jax.ShapeDtypeStruct((b, t, d), F32),
                   jax.ShapeDtypeStruct((N_SHIFT_MIX, b, t, d), BF16)],
        scratch_shapes=[pltpu.VMEM((SUBLANES, d), F32)],
        compiler_params=_params("arbitrary", "arbitrary"),
        name="residual_norm_mix",
    )(x, y, gt, g, sc, sh, mu)


def _final_kernel(x_ref, y_ref, gt_ref, g_ref, o_ref):
    x = x_ref[...] + gt_ref[...] * y_ref[...]
    ms = jnp.mean(x * x, axis=-1, keepdims=True)
    o_ref[...] = (x * lax.rsqrt(ms + NORM_EPS)) * g_ref[...]


def _final(x, y, gt, g):
    b, t, d = x.shape
    tt = _tile(t, ROWS_NORM)
    tile = pl.BlockSpec((None, tt, d), lambda bi, ti: (bi, ti, 0))
    return pl.pallas_call(
        _final_kernel,
        grid=(b, t // tt),
        in_specs=[tile, tile, _row_spec(d), _par_spec(d)],
        out_specs=tile,
        out_shape=jax.ShapeDtypeStruct((b, t, d), F32),
        compiler_params=_params("arbitrary", "arbitrary"),
        name="final_norm",
    )(x, y, gt, g)


def _matmul_kernel(a_ref, w_ref, o_ref):
    o_ref[...] = jnp.dot(a_ref[...], w_ref[...], preferred_element_type=F32)


def _matmul(a, which, w):
    _, m, k = a.shape
    n = w.shape[1]
    tm = _tile(m, TILE_DENSE)
    tn = _tile(n, TILE_DENSE)
    return pl.pallas_call(
        _matmul_kernel,
        grid=(m // tm, n // tn),
        in_specs=[pl.BlockSpec((None, tm, k), lambda i, j: (which, i, 0)),
                  pl.BlockSpec((k, tn), lambda i, j: (0, j))],
        out_specs=pl.BlockSpec((tm, tn), lambda i, j: (i, j)),
        out_shape=jax.ShapeDtypeStruct((m, n), F32),
        compiler_params=_params("arbitrary", "arbitrary"),
        name="dense",
    )(a, w)


def _lora_kernel(x_ref, w1_ref, w2_ref, o_ref, *, act):
    t = jnp.dot(x_ref[...], w1_ref[...], preferred_element_type=F32)
    if act == "tanh":
        t = jnp.tanh(t)
    elif act == "sigmoid":
        t = jax.nn.sigmoid(t)
    o_ref[...] = jnp.dot(t.astype(BF16), w2_ref[...], preferred_element_type=F32)


def _lora(mix, which, w1, w2, act):
    _, m, d = mix.shape
    r = w1.shape[1]
    tm = _tile(m, ROWS_LORA)
    return pl.pallas_call(
        functools.partial(_lora_kernel, act=act),
        grid=(m // tm,),
        in_specs=[pl.BlockSpec((None, tm, d), lambda i: (which, i, 0)),
                  pl.BlockSpec((d, r), lambda i: (0, 0)),
                  pl.BlockSpec((r, d), lambda i: (0, 0))],
        out_specs=pl.BlockSpec((tm, d), lambda i: (i, 0)),
        out_shape=jax.ShapeDtypeStruct((m, d), F32),
        compiler_params=_params("arbitrary"),
        name="lora_" + act,
    )(mix, w1, w2)


def _split(x):
    hi = x.astype(BF16)
    lo = (x - hi.astype(F32)).astype(BF16)
    return hi, lo


def _dg(a, b, ca, cb):
    return lax.dot_general(a, b, (((ca,), (cb,)), ((), ())), preferred_element_type=F32)


def _mm3(a, b, ca, cb):
    a_hi, a_lo = _split(a)
    b_hi, b_lo = _split(b)
    return _dg(jnp.concatenate([a_hi, a_hi, a_lo], axis=ca),
               jnp.concatenate([b_hi, b_lo, b_hi], axis=cb), ca, cb)


def _mm1(a, b, ca, cb):
    return _dg(a.astype(BF16), b.astype(BF16), ca, cb)


_mm = _mm1
_mm_state = _mm1


def _wkv_chunk(rs, ks, vs, wls, als, gs, prms, s0s, c):
    L = rs[0].shape[0]
    P = range(len(rs))
    w0, a0, kkp, kap, rkp, lng, lnb = [[prm[i:i + 1, :] for prm in prms] for i in range(7)]

    def stack(x):
        return jnp.concatenate([jnp.where(c["head0"], x, 0.0), jnp.where(c["head0"], 0.0, x)], axis=0)

    def headsum(xs):
        rows = xs[0].shape[0]
        hi, lo = _split(jnp.concatenate(xs, axis=0))
        sums = _dg(jnp.concatenate([hi, lo], axis=1), c["ones2"], 1, 0)
        return [sums[p * rows:(p + 1) * rows] for p in P]

    def cumsum(xs):
        out = []
        for p in range(0, len(xs), 2):
            hi, lo = _split(jnp.concatenate(xs[p:p + 2], axis=1))
            both = _dg(c["tril2"], jnp.concatenate([hi, lo], axis=0), 1, 0)
            out += [both[:, q * PAIR:(q + 1) * PAIR] for q in range(len(xs[p:p + 2]))]
        return out

    def softplus(z):
        return jnp.maximum(z, 0.0) + jnp.log1p(jnp.exp(-jnp.abs(z)))

    logd = [-jnp.exp(-softplus(-(w0[p] + wls[p])) - 0.5) for p in P]
    lr = [jax.nn.sigmoid(a0[p] + als[p]) for p in P]
    kkr = [ks[p] * kkp[p] for p in P]
    k2 = [ks[p] * (1.0 + (lr[p] - 1.0) * kap[p]) for p in P]
    hs = headsum([jnp.concatenate([kkr[p] * kkr[p], rs[p] * k2[p] * rkp[p]], axis=0) for p in P])
    cum = cumsum(logd)
    kk = [kkr[p] / jnp.maximum(jnp.sqrt(hs[p][:L]), 1e-12) for p in P]
    bonus = [hs[p][L:] for p in P]
    p_t = [jnp.exp(cum[p]) for p in P]
    inv_p = [jnp.exp(-cum[p]) for p in P]
    rt = [rs[p] * p_t[p] for p in P]
    at = [-kk[p] * jnp.exp(cum[p] - logd[p]) for p in P]
    bt = [kk[p] * lr[p] * inv_p[p] for p in P]
    kt = [k2[p] * inv_p[p] for p in P]

    bks = [jnp.concatenate([stack(bt[p]), stack(kt[p])], axis=0) for p in P]
    am = [_mm(jnp.concatenate([at[p], rt[p]], axis=0), bks[p], 1, 1) for p in P]
    n = [jnp.where(c["strict"], am[p][:L, :PAIR], 0.0) for p in P]
    aak = [jnp.where(c["strict"], am[p][:L, PAIR:], 0.0) for p in P]
    mrbk = [jnp.concatenate([jnp.where(c["incl"], am[p][L:, :PAIR], 0.0),
                             jnp.where(c["incl"], am[p][L:, PAIR:], 0.0)], axis=1) for p in P]
    akv = [_mm(aak[p], stack(vs[p]), 1, 0) for p in P]

    npow = [jnp.where(c["diag"], n[p], 0.0) for p in P]
    tinv = [c["eye"] + npow[p] for p in P]
    span = 2
    while span < INV_BASE:
        npow = [_mm(npow[p], stack(npow[p]), 1, 0) for p in P]
        tinv = [tinv[p] + _mm(tinv[p], stack(npow[p]), 1, 0) for p in P]
        span *= 2
    for sub in c["sub"]:
        low = [_mm(jnp.where(sub, n[p], 0.0), stack(tinv[p]), 1, 0) for p in P]
        tinv = [tinv[p] + _mm(tinv[p], stack(low[p]), 1, 0) for p in P]

    tu = [_mm(tinv[p], jnp.concatenate([stack(at[p]), stack(akv[p])], axis=1), 1, 0) for p in P]
    us = [_mm_state(jnp.concatenate([tu[p][:, :PAIR], rt[p]], axis=0), s0s[p], 1, 1) for p in P]
    u = [us[p][:L] + tu[p][:, PAIR:] for p in P]
    p_last = [p_t[p][L - 1:L, :] for p in P]
    s_new = [jnp.where(c["blockdiag"],
                       s0s[p] * p_last[p] + _mm_state(jnp.concatenate([u[p], vs[p]], axis=0),
                                                 jnp.concatenate([bt[p], kt[p]], axis=0) * p_last[p], 0, 0),
                       0.0) for p in P]
    y = [us[p][L:] + _mm1(mrbk[p], jnp.concatenate([stack(u[p]), stack(vs[p])], axis=0), 1, 0) for p in P]

    mean = headsum(y)
    dlt = [y[p] - mean[p] * (1.0 / HEAD) for p in P]
    var = headsum([dlt[p] * dlt[p] for p in P])
    outs = [((dlt[p] * lax.rsqrt(var[p] * (1.0 / HEAD) + LNX_EPS) * lng[p] + lnb[p]) + bonus[p] * vs[p]) * gs[p]
            for p in P]
    return outs, s_new


def _wkv_consts(L):
    lane = lax.broadcasted_iota(jnp.int32, (L, PAIR), 1)
    row = lax.broadcasted_iota(jnp.int32, (L, PAIR), 0)
    src = lane % HEAD
    r2 = lax.broadcasted_iota(jnp.int32, (PAIR, PAIR), 0) // HEAD
    c2 = lax.broadcasted_iota(jnp.int32, (PAIR, PAIR), 1) // HEAD
    blockdiag = r2 == c2
    ones = jnp.where(blockdiag, 1.0, 0.0).astype(BF16)
    tr = lax.broadcasted_iota(jnp.int32, (L, L), 0)
    tc = lax.broadcasted_iota(jnp.int32, (L, L), 1)
    tril = jnp.where(tr >= tc, 1.0, 0.0).astype(BF16)
    sub = []
    q = INV_BASE
    while q < L:
        sub.append((row // (2 * q) == src // (2 * q)) & (row // q == src // q + 1))
        q *= 2
    return {
        "head0": lane < HEAD,
        "strict": row > src,
        "incl": row >= src,
        "eye": jnp.where(row == src, 1.0, 0.0),
        "diag": row // INV_BASE == src // INV_BASE,
        "sub": sub,
        "blockdiag": blockdiag,
        "ones2": jnp.concatenate([ones, ones], axis=0),
        "tril2": jnp.concatenate([tril, tril], axis=1),
    }


def _wkv_kernel(r_ref, k_ref, v_ref, wl_ref, al_ref, g_ref, prm_ref, o_ref, s_ref, *, pairs):
    @pl.when(pl.program_id(2) == 0)
    def _():
        s_ref[...] = jnp.zeros_like(s_ref)

    c = _wkv_consts(r_ref.shape[0])
    cols = [slice(p * PAIR, (p + 1) * PAIR) for p in range(pairs)]
    outs, s_new = _wkv_chunk(*[[ref[:, cs] for cs in cols]
                               for ref in (r_ref, k_ref, v_ref, wl_ref, al_ref, g_ref, prm_ref)],
                             [s_ref[p] for p in range(pairs)], c)
    for p in range(pairs):
        s_ref[p] = s_new[p]
        o_ref[:, cols[p]] = outs[p].astype(o_ref.dtype)


def _wkv(r, k, v, wl, al, g, prm, batch, seq_len):
    m, d = r.shape
    L = _tile(seq_len, CHUNK)
    lanes = _tile(d, LANES_WKV)
    nchunk = seq_len // L
    tok = pl.BlockSpec((L, lanes), lambda b, hg, ci: (b * nchunk + ci, hg))
    return pl.pallas_call(
        functools.partial(_wkv_kernel, pairs=lanes // PAIR),
        grid=(batch, d // lanes, nchunk),
        in_specs=[tok] * 6 + [pl.BlockSpec((SUBLANES, lanes), lambda b, hg, ci: (0, hg))],
        out_specs=tok,
        out_shape=jax.ShapeDtypeStruct((m, d), BF16),
        scratch_shapes=[pltpu.VMEM((lanes // PAIR, PAIR, PAIR), F32)],
        compiler_params=_params("arbitrary", "arbitrary", "arbitrary"),
        name="wkv7_chunk",
    )(r, k, v, wl, al, g, prm)


def _pad_lora(w1, w2):
    r = w1.shape[1]
    rp = -(-r // 128) * 128
    return (jnp.pad(w1, ((0, 0), (0, rp - r))).astype(BF16),
            jnp.pad(w2, ((0, rp - r), (0, 0))).astype(BF16))


def _cast_pad_kernel(x_ref, o_ref, *, nblocks):
    keep = pl.program_id(1) < nblocks
    o_ref[...] = jnp.where(keep, x_ref[...], 0.0).astype(o_ref.dtype)


def _cast_pad(w, axis, start, size, out_size):
    depth = w.shape[0]
    tb = next(t for t in (512, 256, 128) if start % t == 0 and size % t == 0 and out_size % t == 0)
    nblocks, first = size // tb, start // tb
    block = (None, tb, w.shape[2]) if axis == 1 else (None, w.shape[1], tb)

    def src(l, j):
        jj = first + jnp.minimum(j, nblocks - 1)
        return (l, jj, 0) if axis == 1 else (l, 0, jj)

    def dst(l, j):
        return (l, j, 0) if axis == 1 else (l, 0, j)

    out_shape = (depth, out_size, w.shape[2]) if axis == 1 else (depth, w.shape[1], out_size)
    return pl.pallas_call(
        functools.partial(_cast_pad_kernel, nblocks=nblocks),
        grid=(depth, out_size // tb),
        in_specs=[pl.BlockSpec(block, src)],
        out_specs=pl.BlockSpec(block, dst),
        out_shape=jax.ShapeDtypeStruct(out_shape, BF16),
        compiler_params=_params("arbitrary", "arbitrary"),
        name="cast_pad",
    )(w)


def kernel(x, c, ada_w, ada_b, norm_g, pool_w, pool_scale, rwkv_mu, rwkv_wr, rwkv_wk, rwkv_wv,
           rwkv_w0, rwkv_w1, rwkv_w2, rwkv_a0, rwkv_a1, rwkv_a2, rwkv_g1, rwkv_g2, rwkv_kk,
           rwkv_ka, rwkv_rk, rwkv_lnx_g, rwkv_lnx_b, rwkv_wo, ffn_w_up, ffn_conv_w, ffn_conv_b,
           ffn_w_down, final_g):
    b, t, d = x.shape
    m = b * t
    depth = ada_w.shape[0]
    assert depth == 2, "layer 0 is the pooling mixer, layer 1 the RWKV-7 mixer"

    mod = _ada(c, ada_w, ada_b).reshape(depth, b, N_ADA, 1, d)
    sh1, sc1, gt1, sh2, sc2, gt2 = [[mod[l, :, n] for l in range(depth)] for n in range(N_ADA)]
    ng = norm_g.reshape(depth, 2, 1, d)

    f = ffn_w_down.shape[1]
    fpad = -f % min(f, COLS_FFN)
    w_gate = _cast_pad(ffn_w_up, 2, 0, f, f + fpad)
    w_val = _cast_pad(ffn_w_up, 2, f, f, f + fpad)
    w_down = _cast_pad(ffn_w_down, 1, 0, f, f + fpad)
    conv_w = jnp.pad(ffn_conv_w, ((0, 0), (0, 0), (0, fpad)))
    conv_b = jnp.pad(ffn_conv_b, ((0, 0), (0, fpad))).reshape(depth, 1, f + fpad)

    def ffn(h, l):
        return _ffn(h.reshape(m, d), w_gate, w_val, conv_w, conv_b, w_down, l, t).reshape(b, t, d)

    x1, h = _pool_layer(x, ng[0, 0], sc1[0], sh1[0], gt1[0], pool_w[0].astype(BF16),
                        pool_scale[0].reshape(1, d), ng[0, 1], sc2[0], sh2[0])
    y = ffn(h, 0)

    mu = rwkv_mu[0][jnp.array([0, 2, 3, 1, 4, 5])]
    x2, mix = _resnorm_mix(x1, y, gt2[0], ng[1, 0], sc1[1], sh1[1], mu)
    mix = mix.reshape(N_SHIFT_MIX, m, d)
    r, k, v = [_matmul(mix, n, w[0].astype(BF16)) for n, w in enumerate((rwkv_wr, rwkv_wk, rwkv_wv))]
    wl = _lora(mix, 3, *_pad_lora(rwkv_w1[0], rwkv_w2[0]), "tanh")
    al = _lora(mix, 4, *_pad_lora(rwkv_a1[0], rwkv_a2[0]), "none")
    gg = _lora(mix, 5, *_pad_lora(rwkv_g1[0], rwkv_g2[0]), "sigmoid")
    prm = jnp.stack([rwkv_w0[0], rwkv_a0[0], rwkv_kk[0], rwkv_ka[0], rwkv_rk[0].reshape(d),
                     rwkv_lnx_g[0], rwkv_lnx_b[0], jnp.zeros((d,), F32)])
    z = _wkv(r, k, v, wl, al, gg, prm, b, t)
    y = _matmul(z[None], 0, rwkv_wo[0].astype(BF16)).reshape(b, t, d)
    x3, h = _resnorm(x2, y, gt1[1], ng[1, 1], sc2[1], sh2[1])
    y = ffn(h, 1)
    return _final(x3, y, gt2[1], final_g.reshape(1, d))
```

```python
import functools

import jax
import jax.numpy as jnp
from jax import lax
from jax.experimental import pallas as pl
from jax.experimental.pallas import tpu as pltpu

F32 = jnp.float32
BF16 = jnp.bfloat16

POOL_WINDOWS = (2, 4, 8, 16)
POOL_HALO = 16
HEAD = 64
PAIR = 2 * HEAD
CHUNK = 64
INV_BASE = 8
N_SHIFT_MIX = 6
N_ADA = 6
NORM_EPS = 1e-6
LNX_EPS = 64e-5
SUBLANES = 8
LANES = 128
MIX_ROWS = 16
MIX_COLS = 512
VMEM_LIMIT = 60000 * 1024
ROWS_POOL = 256
ROWS_NORM = 256
ROWS_MIX = 128
ROWS_FFN = 1024
COLS_FFN = 512
COLS_FFN_UP = 256
COLS_FFN_OUT = 512
ROWS_LORA = 256
TILE_DENSE = 1024
COLS_ADA = 512
LANES_WKV = 32 * PAIR


def _params(*sem):
    return pltpu.CompilerParams(dimension_semantics=sem, vmem_limit_bytes=VMEM_LIMIT)


def _tile(n, want):
    t = min(n, want)
    assert n % t == 0, (n, want)
    return t


def _ada_kernel(c_ref, w_ref, b_ref, o_ref):
    c = c_ref[...]
    ca = (c * jax.nn.sigmoid(c)).astype(BF16)
    o_ref[...] = jnp.dot(ca, w_ref[...].astype(BF16), preferred_element_type=F32) + b_ref[...]


def _ada(c, ada_w, ada_b):
    depth, d, n = ada_w.shape
    b = c.shape[0]
    cp = jnp.zeros((SUBLANES, d), F32).at[:b].set(c)
    tn = _tile(n, COLS_ADA)
    out = pl.pallas_call(
        _ada_kernel,
        grid=(depth, n // tn),
        in_specs=[
            pl.BlockSpec((SUBLANES, d), lambda l, j: (0, 0)),
            pl.BlockSpec((None, d, tn), lambda l, j: (l, 0, j)),
            pl.BlockSpec((None, 1, tn), lambda l, j: (l, 0, j)),
        ],
        out_specs=pl.BlockSpec((None, SUBLANES, tn), lambda l, j: (l, 0, j)),
        out_shape=jax.ShapeDtypeStruct((depth, SUBLANES, n), F32),
        compiler_params=_params("arbitrary", "arbitrary"),
        name="ada_mod",
    )(cp, ada_w, ada_b.reshape(depth, 1, n))
    return out[:, :b]


def _rms_mod(x, g, sc, sh):
    ms = jnp.mean(x * x, axis=-1, keepdims=True)
    return (x * lax.rsqrt(ms + NORM_EPS)) * (g * (1.0 + sc)) + sh


def _row_spec(d):
    return pl.BlockSpec((None, 1, d), lambda b, t: (b, 0, 0))


def _par_spec(d):
    return pl.BlockSpec((1, d), lambda b, t: (0, 0))


def _pool_kernel(x_ref, g1_ref, sc1_ref, sh1_ref, gt1_ref, pw_ref, ps_ref,
                 g2_ref, sc2_ref, sh2_ref, x1_ref, h2_ref, carry_ref, *, tt, group):
    t = pl.program_id(1)

    @pl.when(t == 0)
    def _():
        carry_ref[...] = jnp.zeros_like(carry_ref)

    x = x_ref[...]
    h = _rms_mod(x, g1_ref[...], sc1_ref[...], sh1_ref[...])
    pos = (t * tt + lax.broadcasted_iota(jnp.int32, (tt, 1), 0) + 1).astype(F32)
    ys = []
    for gi, win in enumerate(POOL_WINDOWS):
        sl = slice(gi * group, (gi + 1) * group)
        hg = h[:, sl]
        s = jnp.concatenate([carry_ref[:, sl], hg], axis=0)
        shift = 1
        while shift < win:
            s = s + pltpu.roll(s, shift, axis=0)
            shift *= 2
        pooled = s[POOL_HALO:, :] / jnp.minimum(pos, float(win)) - hg
        ys.append(jnp.dot(pooled.astype(BF16), pw_ref[gi], preferred_element_type=F32))
    carry_ref[...] = h[tt - POOL_HALO:, :]
    y = jnp.concatenate(ys, axis=1) * ps_ref[...]
    x1 = x + gt1_ref[...] * y
    x1_ref[...] = x1
    h2_ref[...] = _rms_mod(x1, g2_ref[...], sc2_ref[...], sh2_ref[...]).astype(BF16)


def _pool_layer(x, g1, sc1, sh1, gt1, pool_w, pool_scale, g2, sc2, sh2):
    b, t, d = x.shape
    group = d // len(POOL_WINDOWS)
    tt = _tile(t, ROWS_POOL)
    tile = pl.BlockSpec((None, tt, d), lambda bi, ti: (bi, ti, 0))
    return pl.pallas_call(
        functools.partial(_pool_kernel, tt=tt, group=group),
        grid=(b, t // tt),
        in_specs=[tile, _par_spec(d), _row_spec(d), _row_spec(d), _row_spec(d),
                  pl.BlockSpec(pool_w.shape, lambda bi, ti: (0, 0, 0)), _par_spec(d),
                  _par_spec(d), _row_spec(d), _row_spec(d)],
        out_specs=[tile, tile],
        out_shape=[jax.ShapeDtypeStruct((b, t, d), F32), jax.ShapeDtypeStruct((b, t, d), BF16)],
        scratch_shapes=[pltpu.VMEM((POOL_HALO, d), F32)],
        compiler_params=_params("arbitrary", "arbitrary"),
        name="pool_mixer",
    )(x, g1, sc1, sh1, gt1, pool_w, pool_scale, g2, sc2, sh2)


def _gelu(x):
    return 0.5 * x * (1.0 + lax.erf(x * (2.0 ** -0.5)))


def _ffn_kernel(h_ref, wg_ref, wv_ref, cw_ref, cb_ref, wd_ref, y_ref, carry_ref, act_ref,
                *, tm, tn, nf, tiles_per_seq):
    s = pl.program_id(0)

    @pl.when(s == 0)
    def _():
        act_ref[...] = jnp.zeros_like(act_ref)

    act_prev = act_ref[...]
    jd = jnp.maximum(s - 1, 0) % nf
    d = y_ref.shape[-1]
    for n in range(d // tn):
        cols = slice(n * tn, (n + 1) * tn)
        part = jnp.dot(act_prev, wd_ref[:, cols], preferred_element_type=F32)
        y_ref[:, cols] = part + jnp.where(jd == 0, 0.0, y_ref[:, cols])

    su = jnp.minimum(s, pl.num_programs(0) - 2)
    first = (su // nf) % tiles_per_seq == 0
    j = su % nf
    h = h_ref[...]
    tf = act_ref.shape[-1]
    tu = min(tf, COLS_FFN_UP)
    for n in range(tf // tu):
        cols = slice(n * tu, (n + 1) * tu)
        ug = jnp.dot(h, wg_ref[:, cols], preferred_element_type=F32)
        uv = jnp.dot(h, wv_ref[:, cols], preferred_element_type=F32)
        prev = jnp.where(first, 0.0, carry_ref[j, :, cols])
        carry_ref[j, :, cols] = ug[tm - SUBLANES:, :]
        ext = jnp.concatenate([prev, ug], axis=0)
        u1 = ext[SUBLANES - 1:SUBLANES - 1 + tm, :]
        u2 = ext[SUBLANES - 2:SUBLANES - 2 + tm, :]
        gate = (ug * cw_ref[2:3, cols] + u1 * cw_ref[1:2, cols] + u2 * cw_ref[0:1, cols] + cb_ref[:, cols])
        act_ref[:, cols] = (_gelu(gate) * uv).astype(BF16)


def _ffn(h, w_gate, w_val, conv_w, conv_b, w_down, layer, seq_len):
    m, d = h.shape
    f = w_down.shape[1]
    tm = _tile(seq_len, ROWS_FFN)
    tf = _tile(f, COLS_FFN)
    nf = f // tf
    last = (m // tm) * nf - 1

    def up(s):
        return jnp.minimum(s, last)

    def down(s):
        return jnp.maximum(s - 1, 0)

    return pl.pallas_call(
        functools.partial(_ffn_kernel, tm=tm, tn=_tile(d, COLS_FFN_OUT), nf=nf, tiles_per_seq=seq_len // tm),
        grid=(last + 2,),
        in_specs=[
            pl.BlockSpec((tm, d), lambda s: (up(s) // nf, 0), pipeline_mode=pl.Buffered(1)),
            pl.BlockSpec((None, d, tf), lambda s: (layer, 0, up(s) % nf)),
            pl.BlockSpec((None, d, tf), lambda s: (layer, 0, up(s) % nf)),
            pl.BlockSpec((None, conv_w.shape[1], tf), lambda s: (layer, 0, up(s) % nf)),
            pl.BlockSpec((None, 1, tf), lambda s: (layer, 0, up(s) % nf)),
            pl.BlockSpec((None, tf, d), lambda s: (layer, down(s) % nf, 0)),
        ],
        out_specs=pl.BlockSpec((tm, d), lambda s: (down(s) // nf, 0), pipeline_mode=pl.Buffered(1)),
        out_shape=jax.ShapeDtypeStruct((m, d), F32),
        scratch_shapes=[pltpu.VMEM((nf, SUBLANES, tf), F32), pltpu.VMEM((tm, tf), BF16)],
        compiler_params=_params("arbitrary"),
        name="conv_glu",
    )(h, w_gate, w_val, conv_w, conv_b, w_down)


def _resnorm_kernel(x_ref, y_ref, gt_ref, g_ref, sc_ref, sh_ref, xo_ref, h_ref):
    x = x_ref[...] + gt_ref[...] * y_ref[...]
    xo_ref[...] = x
    h_ref[...] = _rms_mod(x, g_ref[...], sc_ref[...], sh_ref[...]).astype(h_ref.dtype)


def _resnorm(x, y, gt, g, sc, sh):
    b, t, d = x.shape
    tt = _tile(t, ROWS_NORM)
    tile = pl.BlockSpec((None, tt, d), lambda bi, ti: (bi, ti, 0))
    return pl.pallas_call(
        _resnorm_kernel,
        grid=(b, t // tt),
        in_specs=[tile, tile, _row_spec(d), _par_spec(d), _row_spec(d), _row_spec(d)],
        out_specs=[tile, tile],
        out_shape=[jax.ShapeDtypeStruct((b, t, d), F32), jax.ShapeDtypeStruct((b, t, d), BF16)],
        compiler_params=_params("arbitrary", "arbitrary"),
        name="residual_norm",
    )(x, y, gt, g, sc, sh)


def _resnorm_mix_kernel(x_ref, y_ref, gt_ref, g_ref, sc_ref, sh_ref, mu_ref, xo_ref, mix_ref,
                        carry_ref, *, tt):
    t = pl.program_id(1)

    @pl.when(t == 0)
    def _():
        carry_ref[...] = jnp.zeros_like(carry_ref)

    d = x_ref.shape[-1]
    cb = min(d, MIX_COLS)
    chunks = [slice(c0, c0 + cb) for c0 in range(0, d, cb)]
    first_row = lax.broadcasted_iota(jnp.int32, (MIX_ROWS, 1), 0) == 0
    last = SUBLANES - 1

    def row_block(r, carry):
        rows = pl.ds(pl.multiple_of(r * MIX_ROWS, MIX_ROWS), MIX_ROWS)
        ss = jnp.zeros((MIX_ROWS, LANES), F32)
        for cols in chunks:
            x = x_ref[rows, cols] + gt_ref[:, cols] * y_ref[rows, cols]
            xo_ref[rows, cols] = x
            xx = x * x
            for l0 in range(0, cb, LANES):
                ss = ss + xx[:, l0:l0 + LANES]
        inv = lax.rsqrt(jnp.sum(ss, axis=-1, keepdims=True) * (1.0 / d) + NORM_EPS)
        for cols in chunks:
            h = (xo_ref[rows, cols] * inv) * (g_ref[:, cols] * (1.0 + sc_ref[:, cols])) + sh_ref[:, cols]
            prev = jnp.where(first_row, carry_ref[last:, cols], pltpu.roll(h, 1, axis=0))
            carry_ref[last:, cols] = h[MIX_ROWS - 1:, :]
            dx = prev - h
            for n in range(N_SHIFT_MIX):
                mix_ref[n, rows, cols] = (h + dx * mu_ref[n:n + 1, cols]).astype(BF16)
        return carry

    lax.fori_loop(0, tt // MIX_ROWS, row_block, 0)


def _resnorm_mix(x, y, gt, g, sc, sh, mu):
    b, t, d = x.shape
    tt = _tile(t, ROWS_MIX)
    tile = pl.BlockSpec((None, tt, d), lambda bi, ti: (bi, ti, 0))
    return pl.pallas_call(
        functools.partial(_resnorm_mix_kernel, tt=tt),
        grid=(b, t // tt),
        in_specs=[tile, tile, _row_spec(d), _par_spec(d), _row_spec(d), _row_spec(d),
                  pl.BlockSpec(mu.shape, lambda bi, ti: (0, 0))],
        out_specs=[tile, pl.BlockSpec((N_SHIFT_MIX, None, tt, d), lambda bi, ti: (0, bi, ti, 0))],
        out_shape=[jax.ShapeDtypeStruct((b, t, d), F32),
                   jax.ShapeDtypeStruct((N_SHIFT_MIX, b, t, d), BF16)],
        scratch_shapes=[pltpu.VMEM((SUBLANES, d), F32)],
        compiler_params=_params("arbitrary", "arbitrary"),
        name="residual_norm_mix",
    )(x, y, gt, g, sc, sh, mu)


def _final_kernel(x_ref, y_ref, gt_ref, g_ref, o_ref):
    x = x_ref[...] + gt_ref[...] * y_ref[...]
    ms = jnp.mean(x * x, axis=-1, keepdims=True)
    o_ref[...] = (x * lax.rsqrt(ms + NORM_EPS)) * g_ref[...]


def _final(x, y, gt, g):
    b, t, d = x.shape
    tt = _tile(t, ROWS_NORM)
    tile = pl.BlockSpec((None, tt, d), lambda bi, ti: (bi, ti, 0))
    return pl.pallas_call(
        _final_kernel,
        grid=(b, t // tt),
        in_specs=[tile, tile, _row_spec(d), _par_spec(d)],
        out_specs=tile,
        out_shape=jax.ShapeDtypeStruct((b, t, d), F32),
        compiler_params=_params("arbitrary", "arbitrary"),
        name="final_norm",
    )(x, y, gt, g)


def _matmul_kernel(a_ref, w_ref, o_ref):
    o_ref[...] = jnp.dot(a_ref[...], w_ref[...], preferred_element_type=F32)


def _matmul(a, which, w):
    _, m, k = a.shape
    n = w.shape[1]
    tm = _tile(m, TILE_DENSE)
    tn = _tile(n, TILE_DENSE)
    return pl.pallas_call(
        _matmul_kernel,
        grid=(m // tm, n // tn),
        in_specs=[pl.BlockSpec((None, tm, k), lambda i, j: (which, i, 0)),
                  pl.BlockSpec((k, tn), lambda i, j: (0, j))],
        out_specs=pl.BlockSpec((tm, tn), lambda i, j: (i, j)),
        out_shape=jax.ShapeDtypeStruct((m, n), F32),
        compiler_params=_params("arbitrary", "arbitrary"),
        name="dense",
    )(a, w)


def _lora_kernel(x_ref, w1_ref, w2_ref, o_ref, *, act):
    t = jnp.dot(x_ref[...], w1_ref[...], preferred_element_type=F32)
    if act == "tanh":
        t = jnp.tanh(t)
    elif act == "sigmoid":
        t = jax.nn.sigmoid(t)
    o_ref[...] = jnp.dot(t.astype(BF16), w2_ref[...], preferred_element_type=F32)


def _lora(mix, which, w1, w2, act):
    _, m, d = mix.shape
    r = w1.shape[1]
    tm = _tile(m, ROWS_LORA)
    return pl.pallas_call(
        functools.partial(_lora_kernel, act=act),
        grid=(m // tm,),
        in_specs=[pl.BlockSpec((None, tm, d), lambda i: (which, i, 0)),
                  pl.BlockSpec((d, r), lambda i: (0, 0)),
                  pl.BlockSpec((r, d), lambda i: (0, 0))],
        out_specs=pl.BlockSpec((tm, d), lambda i: (i, 0)),
        out_shape=jax.ShapeDtypeStruct((m, d), F32),
        compiler_params=_params("arbitrary"),
        name="lora_" + act,
    )(mix, w1, w2)


def _split(x):
    hi = x.astype(BF16)
    lo = (x - hi.astype(F32)).astype(BF16)
    return hi, lo


def _dg(a, b, ca, cb):
    return lax.dot_general(a, b, (((ca,), (cb,)), ((), ())), preferred_element_type=F32)


def _mm(a, b, ca, cb):
    return _dg(a.astype(BF16), b.astype(BF16), ca, cb)


def _wkv_chunk(rs, ks, vs, wls, als, gs, prms, s0s, c):
    L = rs[0].shape[0]
    P = range(len(rs))
    w0, a0, kkp, kap, rkp, lng, lnb = [[prm[i:i + 1, :] for prm in prms] for i in range(7)]

    def stack(x):
        return jnp.concatenate([jnp.where(c["head0"], x, 0.0), jnp.where(c["head0"], 0.0, x)], axis=0)

    def headsum(xs):
        rows = xs[0].shape[0]
        hi, lo = _split(jnp.concatenate(xs, axis=0))
        sums = _dg(jnp.concatenate([hi, lo], axis=1), c["ones2"], 1, 0)
        return [sums[p * rows:(p + 1) * rows] for p in P]

    def cumsum(xs):
        out = []
        for p in range(0, len(xs), 2):
            hi, lo = _split(jnp.concatenate(xs[p:p + 2], axis=1))
            both = _dg(c["tril2"], jnp.concatenate([hi, lo], axis=0), 1, 0)
            out += [both[:, q * PAIR:(q + 1) * PAIR] for q in range(len(xs[p:p + 2]))]
        return out

    def softplus(z):
        return jnp.maximum(z, 0.0) + jnp.log1p(jnp.exp(-jnp.abs(z)))

    logd = [-jnp.exp(-softplus(-(w0[p] + wls[p])) - 0.5) for p in P]
    lr = [jax.nn.sigmoid(a0[p] + als[p]) for p in P]
    kkr = [ks[p] * kkp[p] for p in P]
    k2 = [ks[p] * (1.0 + (lr[p] - 1.0) * kap[p]) for p in P]
    hs = headsum([jnp.concatenate([kkr[p] * kkr[p], rs[p] * k2[p] * rkp[p]], axis=0) for p in P])
    cum = cumsum(logd)
    kk = [kkr[p] / jnp.maximum(jnp.sqrt(hs[p][:L]), 1e-12) for p in P]
    bonus = [hs[p][L:] for p in P]
    p_t = [jnp.exp(cum[p]) for p in P]
    inv_p = [jnp.exp(-cum[p]) for p in P]
    rt = [rs[p] * p_t[p] for p in P]
    at = [-kk[p] * jnp.exp(cum[p] - logd[p]) for p in P]
    bt = [kk[p] * lr[p] * inv_p[p] for p in P]
    kt = [k2[p] * inv_p[p] for p in P]

    bks = [jnp.concatenate([stack(bt[p]), stack(kt[p])], axis=0) for p in P]
    am = [_mm(jnp.concatenate([at[p], rt[p]], axis=0), bks[p], 1, 1) for p in P]
    n = [jnp.where(c["strict"], am[p][:L, :PAIR], 0.0) for p in P]
    aak = [jnp.where(c["strict"], am[p][:L, PAIR:], 0.0) for p in P]
    mrbk = [jnp.concatenate([jnp.where(c["incl"], am[p][L:, :PAIR], 0.0),
                             jnp.where(c["incl"], am[p][L:, PAIR:], 0.0)], axis=1) for p in P]
    akv = [_mm(aak[p], stack(vs[p]), 1, 0) for p in P]

    npow = [jnp.where(c["diag"], n[p], 0.0) for p in P]
    tinv = [c["eye"] + npow[p] for p in P]
    span = 2
    while span < INV_BASE:
        npow = [_mm(npow[p], stack(npow[p]), 1, 0) for p in P]
        tinv = [tinv[p] + _mm(tinv[p], stack(npow[p]), 1, 0) for p in P]
        span *= 2
    for sub in c["sub"]:
        low = [_mm(jnp.where(sub, n[p], 0.0), stack(tinv[p]), 1, 0) for p in P]
        tinv = [tinv[p] + _mm(tinv[p], stack(low[p]), 1, 0) for p in P]

    tu = [_mm(tinv[p], jnp.concatenate([stack(at[p]), stack(akv[p])], axis=1), 1, 0) for p in P]
    us = [_mm(jnp.concatenate([tu[p][:, :PAIR], rt[p]], axis=0), s0s[p], 1, 1) for p in P]
    u = [us[p][:L] + tu[p][:, PAIR:] for p in P]
    p_last = [p_t[p][L - 1:L, :] for p in P]
    s_new = [jnp.where(c["blockdiag"],
                       s0s[p] * p_last[p] + _mm(jnp.concatenate([u[p], vs[p]], axis=0),
                                                 jnp.concatenate([bt[p], kt[p]], axis=0) * p_last[p], 0, 0),
                       0.0) for p in P]
    y = [us[p][L:] + _mm(mrbk[p], jnp.concatenate([stack(u[p]), stack(vs[p])], axis=0), 1, 0) for p in P]

    mean = headsum(y)
    dlt = [y[p] - mean[p] * (1.0 / HEAD) for p in P]
    var = headsum([dlt[p] * dlt[p] for p in P])
    outs = [((dlt[p] * lax.rsqrt(var[p] * (1.0 / HEAD) + LNX_EPS) * lng[p] + lnb[p]) + bonus[p] * vs[p]) * gs[p]
            for p in P]
    return outs, s_new


def _wkv_consts(L):
    lane = lax.broadcasted_iota(jnp.int32, (L, PAIR), 1)
    row = lax.broadcasted_iota(jnp.int32, (L, PAIR), 0)
    src = lane % HEAD
    r2 = lax.broadcasted_iota(jnp.int32, (PAIR, PAIR), 0) // HEAD
    c2 = lax.broadcasted_iota(jnp.int32, (PAIR, PAIR), 1) // HEAD
    blockdiag = r2 == c2
    ones = jnp.where(blockdiag, 1.0, 0.0).astype(BF16)
    tr = lax.broadcasted_iota(jnp.int32, (L, L), 0)
    tc = lax.broadcasted_iota(jnp.int32, (L, L), 1)
    tril = jnp.where(tr >= tc, 1.0, 0.0).astype(BF16)
    sub = []
    q = INV_BASE
    while q < L:
        sub.append((row // (2 * q) == src // (2 * q)) & (row // q == src // q + 1))
        q *= 2
    return {
        "head0": lane < HEAD,
        "strict": row > src,
        "incl": row >= src,
        "eye": jnp.where(row == src, 1.0, 0.0),
        "diag": row // INV_BASE == src // INV_BASE,
        "sub": sub,
        "blockdiag": blockdiag,
        "ones2": jnp.concatenate([ones, ones], axis=0),
        "tril2": jnp.concatenate([tril, tril], axis=1),
    }


def _wkv_kernel(r_ref, k_ref, v_ref, wl_ref, al_ref, g_ref, prm_ref, o_ref, s_ref, *, pairs):
    @pl.when(pl.program_id(2) == 0)
    def _():
        s_ref[...] = jnp.zeros_like(s_ref)

    c = _wkv_consts(r_ref.shape[0])
    cols = [slice(p * PAIR, (p + 1) * PAIR) for p in range(pairs)]
    outs, s_new = _wkv_chunk(*[[ref[:, cs] for cs in cols]
                               for ref in (r_ref, k_ref, v_ref, wl_ref, al_ref, g_ref, prm_ref)],
                             [s_ref[p] for p in range(pairs)], c)
    for p in range(pairs):
        s_ref[p] = s_new[p]
        o_ref[:, cols[p]] = outs[p].astype(o_ref.dtype)


def _wkv(r, k, v, wl, al, g, prm, batch, seq_len):
    m, d = r.shape
    L = _tile(seq_len, CHUNK)
    lanes = _tile(d, LANES_WKV)
    nchunk = seq_len // L
    tok = pl.BlockSpec((L, lanes), lambda b, hg, ci: (b * nchunk + ci, hg))
    return pl.pallas_call(
        functools.partial(_wkv_kernel, pairs=lanes // PAIR),
        grid=(batch, d // lanes, nchunk),
        in_specs=[tok] * 6 + [pl.BlockSpec((SUBLANES, lanes), lambda b, hg, ci: (0, hg))],
        out_specs=tok,
        out_shape=jax.ShapeDtypeStruct((m, d), BF16),
        scratch_shapes=[pltpu.VMEM((lanes // PAIR, PAIR, PAIR), F32)],
        compiler_params=_params("arbitrary", "arbitrary", "arbitrary"),
        name="wkv7_chunk",
    )(r, k, v, wl, al, g, prm)


def _pad_lora(w1, w2):
    r = w1.shape[1]
    rp = -(-r // 128) * 128
    return (jnp.pad(w1, ((0, 0), (0, rp - r))).astype(BF16),
            jnp.pad(w2, ((0, rp - r), (0, 0))).astype(BF16))


def _cast_pad_kernel(x_ref, o_ref, *, nblocks):
    keep = pl.program_id(1) < nblocks
    o_ref[...] = jnp.where(keep, x_ref[...], 0.0).astype(o_ref.dtype)


def _cast_pad(w, axis, start, size, out_size):
    depth = w.shape[0]
    tb = next(t for t in (512, 256, 128) if start % t == 0 and size % t == 0 and out_size % t == 0)
    nblocks, first = size // tb, start // tb
    block = (None, tb, w.shape[2]) if axis == 1 else (None, w.shape[1], tb)

    def src(l, j):
        jj = first + jnp.minimum(j, nblocks - 1)
        return (l, jj, 0) if axis == 1 else (l, 0, jj)

    def dst(l, j):
        return (l, j, 0) if axis == 1 else (l, 0, j)

    out_shape = (depth, out_size, w.shape[2]) if axis == 1 else (depth, w.shape[1], out_size)
    return pl.pallas_call(
        functools.partial(_cast_pad_kernel, nblocks=nblocks),
        grid=(depth, out_size // tb),
        in_specs=[pl.BlockSpec(block, src)],
        out_specs=pl.BlockSpec(block, dst),
        out_shape=jax.ShapeDtypeStruct(out_shape, BF16),
        compiler_params=_params("arbitrary", "arbitrary"),
        name="cast_pad",
    )(w)


def kernel(x, c, ada_w, ada_b, norm_g, pool_w, pool_scale, rwkv_mu, rwkv_wr, rwkv_wk, rwkv_wv,
           rwkv_w0, rwkv_w1, rwkv_w2, rwkv_a0, rwkv_a1, rwkv_a2, rwkv_g1, rwkv_g2, rwkv_kk,
           rwkv_ka, rwkv_rk, rwkv_lnx_g, rwkv_lnx_b, rwkv_wo, ffn_w_up, ffn_conv_w, ffn_conv_b,
           ffn_w_down, final_g):
    b, t, d = x.shape
    m = b * t
    depth = ada_w.shape[0]
    assert depth == 2, "layer 0 is the pooling mixer, layer 1 the RWKV-7 mixer"

    mod = _ada(c, ada_w, ada_b).reshape(depth, b, N_ADA, 1, d)
    sh1, sc1, gt1, sh2, sc2, gt2 = [[mod[l, :, n] for l in range(depth)] for n in range(N_ADA)]
    ng = norm_g.reshape(depth, 2, 1, d)

    f = ffn_w_down.shape[1]
    fpad = -f % min(f, COLS_FFN)
    w_gate = _cast_pad(ffn_w_up, 2, 0, f, f + fpad)
    w_val = _cast_pad(ffn_w_up, 2, f, f, f + fpad)
    w_down = _cast_pad(ffn_w_down, 1, 0, f, f + fpad)
    conv_w = jnp.pad(ffn_conv_w, ((0, 0), (0, 0), (0, fpad)))
    conv_b = jnp.pad(ffn_conv_b, ((0, 0), (0, fpad))).reshape(depth, 1, f + fpad)

    def ffn(h, l):
        return _ffn(h.reshape(m, d), w_gate, w_val, conv_w, conv_b, w_down, l, t).reshape(b, t, d)

    x1, h = _pool_layer(x, ng[0, 0], sc1[0], sh1[0], gt1[0], pool_w[0].astype(BF16),
                        pool_scale[0].reshape(1, d), ng[0, 1], sc2[0], sh2[0])
    y = ffn(h, 0)

    mu = rwkv_mu[0][jnp.array([0, 2, 3, 1, 4, 5])]
    x2, mix = _resnorm_mix(x1, y, gt2[0], ng[1, 0], sc1[1], sh1[1], mu)
    mix = mix.reshape(N_SHIFT_MIX, m, d)
    r, k, v = [_matmul(mix, n, w[0].astype(BF16)) for n, w in enumerate((rwkv_wr, rwkv_wk, rwkv_wv))]
    wl = _lora(mix, 3, *_pad_lora(rwkv_w1[0], rwkv_w2[0]), "tanh")
    al = _lora(mix, 4, *_pad_lora(rwkv_a1[0], rwkv_a2[0]), "none")
    gg = _lora(mix, 5, *_pad_lora(rwkv_g1[0], rwkv_g2[0]), "sigmoid")
    prm = jnp.stack([rwkv_w0[0], rwkv_a0[0], rwkv_kk[0], rwkv_ka[0], rwkv_rk[0].reshape(d),
                     rwkv_lnx_g[0], rwkv_lnx_b[0], jnp.zeros((d,), F32)])
    z = _wkv(r, k, v, wl, al, gg, prm, b, t)
    y = _matmul(z[None], 0, rwkv_wo[0].astype(BF16)).reshape(b, t, d)
    x3, h = _resnorm(x2, y, gt1[1], ng[1, 1], sc2[1], sh2[1])
    y = ffn(h, 1)
    return _final(x3, y, gt2[1], final_g.reshape(1, d))
```

```python
import functools

import jax
import jax.numpy as jnp
from jax import lax
from jax.experimental import pallas as pl
from jax.experimental.pallas import tpu as pltpu

F32 = jnp.float32
BF16 = jnp.bfloat16

POOL_WINDOWS = (2, 4, 8, 16)
POOL_HALO = 16
HEAD = 64
PAIR = 2 * HEAD
CHUNK = 64
INV_BASE = 8
N_SHIFT_MIX = 6
N_ADA = 6
NORM_EPS = 1e-6
LNX_EPS = 64e-5
SUBLANES = 8
LANES = 128
MIX_ROWS = 16
MIX_COLS = 512
VMEM_LIMIT = 60000 * 1024
ROWS_POOL = 256
ROWS_NORM = 256
ROWS_MIX = 256
ROWS_FFN = 1024
COLS_FFN = 512
COLS_FFN_UP = 256
COLS_FFN_OUT = 512
ROWS_LORA = 256
TILE_DENSE = 1024
COLS_ADA = 512
LANES_WKV = 32 * PAIR


def _params(*sem):
    return pltpu.CompilerParams(dimension_semantics=sem, vmem_limit_bytes=VMEM_LIMIT)


def _tile(n, want):
    t = min(n, want)
    assert n % t == 0, (n, want)
    return t


def _ada_kernel(c_ref, w_ref, b_ref, o_ref):
    c = c_ref[...]
    ca = (c * jax.nn.sigmoid(c)).astype(BF16)
    o_ref[...] = jnp.dot(ca, w_ref[...].astype(BF16), preferred_element_type=F32) + b_ref[...]


def _ada(c, ada_w, ada_b):
    depth, d, n = ada_w.shape
    b = c.shape[0]
    cp = jnp.zeros((SUBLANES, d), F32).at[:b].set(c)
    tn = _tile(n, COLS_ADA)
    out = pl.pallas_call(
        _ada_kernel,
        grid=(depth, n // tn),
        in_specs=[
            pl.BlockSpec((SUBLANES, d), lambda l, j: (0, 0)),
            pl.BlockSpec((None, d, tn), lambda l, j: (l, 0, j)),
            pl.BlockSpec((None, 1, tn), lambda l, j: (l, 0, j)),
        ],
        out_specs=pl.BlockSpec((None, SUBLANES, tn), lambda l, j: (l, 0, j)),
        out_shape=jax.ShapeDtypeStruct((depth, SUBLANES, n), F32),
        compiler_params=_params("arbitrary", "arbitrary"),
        name="ada_mod",
    )(cp, ada_w, ada_b.reshape(depth, 1, n))
    return out[:, :b]


def _rms_mod(x, g, sc, sh):
    ms = jnp.mean(x * x, axis=-1, keepdims=True)
    return (x * lax.rsqrt(ms + NORM_EPS)) * (g * (1.0 + sc)) + sh


def _row_spec(d):
    return pl.BlockSpec((None, 1, d), lambda b, t: (b, 0, 0))


def _par_spec(d):
    return pl.BlockSpec((1, d), lambda b, t: (0, 0))


def _pool_kernel(x_ref, g1_ref, sc1_ref, sh1_ref, gt1_ref, pw_ref, ps_ref,
                 g2_ref, sc2_ref, sh2_ref, x1_ref, h2_ref, carry_ref, *, tt, group):
    t = pl.program_id(1)

    @pl.when(t == 0)
    def _():
        carry_ref[...] = jnp.zeros_like(carry_ref)

    x = x_ref[...]
    h = _rms_mod(x, g1_ref[...], sc1_ref[...], sh1_ref[...])
    pos = (t * tt + lax.broadcasted_iota(jnp.int32, (tt, 1), 0) + 1).astype(F32)
    ys = []
    for gi, win in enumerate(POOL_WINDOWS):
        sl = slice(gi * group, (gi + 1) * group)
        hg = h[:, sl]
        s = jnp.concatenate([carry_ref[:, sl], hg], axis=0)
        shift = 1
        while shift < win:
            s = s + pltpu.roll(s, shift, axis=0)
            shift *= 2
        pooled = s[POOL_HALO:, :] / jnp.minimum(pos, float(win)) - hg
        ys.append(jnp.dot(pooled.astype(BF16), pw_ref[gi], preferred_element_type=F32))
    carry_ref[...] = h[tt - POOL_HALO:, :]
    y = jnp.concatenate(ys, axis=1) * ps_ref[...]
    x1 = x + gt1_ref[...] * y
    x1_ref[...] = x1
    h2_ref[...] = _rms_mod(x1, g2_ref[...], sc2_ref[...], sh2_ref[...]).astype(BF16)


def _pool_layer(x, g1, sc1, sh1, gt1, pool_w, pool_scale, g2, sc2, sh2):
    b, t, d = x.shape
    group = d // len(POOL_WINDOWS)
    tt = _tile(t, ROWS_POOL)
    tile = pl.BlockSpec((None, tt, d), lambda bi, ti: (bi, ti, 0))
    return pl.pallas_call(
        functools.partial(_pool_kernel, tt=tt, group=group),
        grid=(b, t // tt),
        in_specs=[tile, _par_spec(d), _row_spec(d), _row_spec(d), _row_spec(d),
                  pl.BlockSpec(pool_w.shape, lambda bi, ti: (0, 0, 0)), _par_spec(d),
                  _par_spec(d), _row_spec(d), _row_spec(d)],
        out_specs=[tile, tile],
        out_shape=[jax.ShapeDtypeStruct((b, t, d), F32), jax.ShapeDtypeStruct((b, t, d), BF16)],
        scratch_shapes=[pltpu.VMEM((POOL_HALO, d), F32)],
        compiler_params=_params("arbitrary", "arbitrary"),
        name="pool_mixer",
    )(x, g1, sc1, sh1, gt1, pool_w, pool_scale, g2, sc2, sh2)


def _gelu(x):
    return 0.5 * x * (1.0 + lax.erf(x * (2.0 ** -0.5)))


def _ffn_kernel(h_ref, wg_ref, wv_ref, cw_ref, cb_ref, wd_ref, y_ref, carry_ref, act_ref,
                *, tm, tn, nf, tiles_per_seq):
    s = pl.program_id(0)

    @pl.when(s == 0)
    def _():
        act_ref[...] = jnp.zeros_like(act_ref)

    act_prev = act_ref[...]
    jd = jnp.maximum(s - 1, 0) % nf
    d = y_ref.shape[-1]
    for n in range(d // tn):
        cols = slice(n * tn, (n + 1) * tn)
        part = jnp.dot(act_prev, wd_ref[:, cols], preferred_element_type=F32)
        y_ref[:, cols] = part + jnp.where(jd == 0, 0.0, y_ref[:, cols])

    su = jnp.minimum(s, pl.num_programs(0) - 2)
    first = (su // nf) % tiles_per_seq == 0
    j = su % nf
    h = h_ref[...]
    tf = act_ref.shape[-1]
    tu = min(tf, COLS_FFN_UP)
    for n in range(tf // tu):
        cols = slice(n * tu, (n + 1) * tu)
        ug = jnp.dot(h, wg_ref[:, cols], preferred_element_type=F32)
        uv = jnp.dot(h, wv_ref[:, cols], preferred_element_type=F32)
        prev = jnp.where(first, 0.0, carry_ref[j, :, cols])
        carry_ref[j, :, cols] = ug[tm - SUBLANES:, :]
        ext = jnp.concatenate([prev, ug], axis=0)
        u1 = ext[SUBLANES - 1:SUBLANES - 1 + tm, :]
        u2 = ext[SUBLANES - 2:SUBLANES - 2 + tm, :]
        gate = (ug * cw_ref[2:3, cols] + u1 * cw_ref[1:2, cols] + u2 * cw_ref[0:1, cols] + cb_ref[:, cols])
        act_ref[:, cols] = (_gelu(gate) * uv).astype(BF16)


def _ffn(h, w_gate, w_val, conv_w, conv_b, w_down, layer, seq_len):
    m, d = h.shape
    f = w_down.shape[1]
    tm = _tile(seq_len, ROWS_FFN)
    tf = _tile(f, COLS_FFN)
    nf = f // tf
    last = (m // tm) * nf - 1

    def up(s):
        return jnp.minimum(s, last)

    def down(s):
        return jnp.maximum(s - 1, 0)

    return pl.pallas_call(
        functools.partial(_ffn_kernel, tm=tm, tn=_tile(d, COLS_FFN_OUT), nf=nf, tiles_per_seq=seq_len // tm),
        grid=(last + 2,),
        in_specs=[
            pl.BlockSpec((tm, d), lambda s: (up(s) // nf, 0), pipeline_mode=pl.Buffered(1)),
            pl.BlockSpec((None, d, tf), lambda s: (layer, 0, up(s) % nf)),
            pl.BlockSpec((None, d, tf), lambda s: (layer, 0, up(s) % nf)),
            pl.BlockSpec((None, conv_w.shape[1], tf), lambda s: (layer, 0, up(s) % nf)),
            pl.BlockSpec((None, 1, tf), lambda s: (layer, 0, up(s) % nf)),
            pl.BlockSpec((None, tf, d), lambda s: (layer, down(s) % nf, 0)),
        ],
        out_specs=pl.BlockSpec((tm, d), lambda s: (down(s) // nf, 0), pipeline_mode=pl.Buffered(1)),
        out_shape=jax.ShapeDtypeStruct((m, d), F32),
        scratch_shapes=[pltpu.VMEM((nf, SUBLANES, tf), F32), pltpu.VMEM((tm, tf), BF16)],
        compiler_params=_params("arbitrary"),
        name="conv_glu",
    )(h, w_gate, w_val, conv_w, conv_b, w_down)


def _resnorm_kernel(x_ref, y_ref, gt_ref, g_ref, sc_ref, sh_ref, xo_ref, h_ref):
    x = x_ref[...] + gt_ref[...] * y_ref[...]
    xo_ref[...] = x
    h_ref[...] = _rms_mod(x, g_ref[...], sc_ref[...], sh_ref[...]).astype(h_ref.dtype)


def _resnorm(x, y, gt, g, sc, sh):
    b, t, d = x.shape
    tt = _tile(t, ROWS_NORM)
    tile = pl.BlockSpec((None, tt, d), lambda bi, ti: (bi, ti, 0))
    return pl.pallas_call(
        _resnorm_kernel,
        grid=(b, t // tt),
        in_specs=[tile, tile, _row_spec(d), _par_spec(d), _row_spec(d), _row_spec(d)],
        out_specs=[tile, tile],
        out_shape=[jax.ShapeDtypeStruct((b, t, d), F32), jax.ShapeDtypeStruct((b, t, d), BF16)],
        compiler_params=_params("arbitrary", "arbitrary"),
        name="residual_norm",
    )(x, y, gt, g, sc, sh)


def _resnorm_mix_kernel(x_ref, y_ref, gt_ref, g_ref, sc_ref, sh_ref, mu_ref, xo_ref, mix_ref,
                        carry_ref, *, tt):
    t = pl.program_id(1)

    @pl.when(t == 0)
    def _():
        carry_ref[...] = jnp.zeros_like(carry_ref)

    d = x_ref.shape[-1]
    cb = min(d, MIX_COLS)
    chunks = [slice(c0, c0 + cb) for c0 in range(0, d, cb)]
    first_row = lax.broadcasted_iota(jnp.int32, (MIX_ROWS, 1), 0) == 0
    last = SUBLANES - 1

    def row_block(r, carry):
        rows = pl.ds(pl.multiple_of(r * MIX_ROWS, MIX_ROWS), MIX_ROWS)
        ss = jnp.zeros((MIX_ROWS, LANES), F32)
        for cols in chunks:
            x = x_ref[rows, cols] + gt_ref[:, cols] * y_ref[rows, cols]
            xo_ref[rows, cols] = x
            xx = x * x
            for l0 in range(0, cb, LANES):
                ss = ss + xx[:, l0:l0 + LANES]
        inv = lax.rsqrt(jnp.sum(ss, axis=-1, keepdims=True) * (1.0 / d) + NORM_EPS)
        for cols in chunks:
            h = (xo_ref[rows, cols] * inv) * (g_ref[:, cols] * (1.0 + sc_ref[:, cols])) + sh_ref[:, cols]
            prev = jnp.where(first_row, carry_ref[last:, cols], pltpu.roll(h, 1, axis=0))
            carry_ref[last:, cols] = h[MIX_ROWS - 1:, :]
            dx = prev - h
            for n in range(N_SHIFT_MIX):
                mix_ref[n, rows, cols] = (h + dx * mu_ref[n:n + 1, cols]).astype(BF16)
        return carry

    lax.fori_loop(0, tt // MIX_ROWS, row_block, 0)


def _resnorm_mix(x, y, gt, g, sc, sh, mu):
    b, t, d = x.shape
    tt = _tile(t, ROWS_MIX)
    tile = pl.BlockSpec((None, tt, d), lambda bi, ti: (bi, ti, 0))
    return pl.pallas_call(
        functools.partial(_resnorm_mix_kernel, tt=tt),
        grid=(b, t // tt),
        in_specs=[tile, tile, _row_spec(d), _par_spec(d), _row_spec(d), _row_spec(d),
                  pl.BlockSpec(mu.shape, lambda bi, ti: (0, 0))],
        out_specs=[tile, pl.BlockSpec((N_SHIFT_MIX, None, tt, d), lambda bi, ti: (0, bi, ti, 0))],
        out_shape=[jax.ShapeDtypeStruct((b, t, d), F32),
                   jax.ShapeDtypeStruct((N_SHIFT_MIX, b, t, d), BF16)],
        scratch_shapes=[pltpu.VMEM((SUBLANES, d), F32)],
        compiler_params=_params("arbitrary", "arbitrary"),
        name="residual_norm_mix",
    )(x, y, gt, g, sc, sh, mu)


def _final_kernel(x_ref, y_ref, gt_ref, g_ref, o_ref):
    x = x_ref[...] + gt_ref[...] * y_ref[...]
    ms = jnp.mean(x * x, axis=-1, keepdims=True)
    o_ref[...] = (x * lax.rsqrt(ms + NORM_EPS)) * g_ref[...]


def _final(x, y, gt, g):
    b, t, d = x.shape
    tt = _tile(t, ROWS_NORM)
    tile = pl.BlockSpec((None, tt, d), lambda bi, ti: (bi, ti, 0))
    return pl.pallas_call(
        _final_kernel,
        grid=(b, t // tt),
        in_specs=[tile, tile, _row_spec(d), _par_spec(d)],
        out_specs=tile,
        out_shape=jax.ShapeDtypeStruct((b, t, d), F32),
        compiler_params=_params("arbitrary", "arbitrary"),
        name="final_norm",
    )(x, y, gt, g)


def _matmul_kernel(a_ref, w_ref, o_ref):
    o_ref[...] = jnp.dot(a_ref[...], w_ref[...], preferred_element_type=F32)


def _matmul(a, which, w):
    _, m, k = a.shape
    n = w.shape[1]
    tm = _tile(m, TILE_DENSE)
    tn = _tile(n, TILE_DENSE)
    return pl.pallas_call(
        _matmul_kernel,
        grid=(m // tm, n // tn),
        in_specs=[pl.BlockSpec((None, tm, k), lambda i, j: (which, i, 0)),
                  pl.BlockSpec((k, tn), lambda i, j: (0, j))],
        out_specs=pl.BlockSpec((tm, tn), lambda i, j: (i, j)),
        out_shape=jax.ShapeDtypeStruct((m, n), F32),
        compiler_params=_params("arbitrary", "arbitrary"),
        name="dense",
    )(a, w)


def _lora_kernel(x_ref, w1_ref, w2_ref, o_ref, *, act):
    t = jnp.dot(x_ref[...], w1_ref[...], preferred_element_type=F32)
    if act == "tanh":
        t = jnp.tanh(t)
    elif act == "sigmoid":
        t = jax.nn.sigmoid(t)
    o_ref[...] = jnp.dot(t.astype(BF16), w2_ref[...], preferred_element_type=F32)


def _lora(mix, which, w1, w2, act):
    _, m, d = mix.shape
    r = w1.shape[1]
    tm = _tile(m, ROWS_LORA)
    return pl.pallas_call(
        functools.partial(_lora_kernel, act=act),
        grid=(m // tm,),
        in_specs=[pl.BlockSpec((None, tm, d), lambda i: (which, i, 0)),
                  pl.BlockSpec((d, r), lambda i: (0, 0)),
                  pl.BlockSpec((r, d), lambda i: (0, 0))],
        out_specs=pl.BlockSpec((tm, d), lambda i: (i, 0)),
        out_shape=jax.ShapeDtypeStruct((m, d), F32),
        compiler_params=_params("arbitrary"),
        name="lora_" + act,
    )(mix, w1, w2)


def _split(x):
    hi = x.astype(BF16)
    lo = (x - hi.astype(F32)).astype(BF16)
    return hi, lo


def _dg(a, b, ca, cb):
    return lax.dot_general(a, b, (((ca,), (cb,)), ((), ())), preferred_element_type=F32)


def _mm(a, b, ca, cb):
    return _dg(a.astype(BF16), b.astype(BF16), ca, cb)


def _wkv_chunk(rs, ks, vs, wls, als, gs, prms, s0s, c):
    L = rs[0].shape[0]
    P = range(len(rs))
    w0, a0, kkp, kap, rkp, lng, lnb = [[prm[i:i + 1, :] for prm in prms] for i in range(7)]

    def stack(x):
        return jnp.concatenate([jnp.where(c["head0"], x, 0.0), jnp.where(c["head0"], 0.0, x)], axis=0)

    def headsum(xs):
        rows = xs[0].shape[0]
        hi, lo = _split(jnp.concatenate(xs, axis=0))
        sums = _dg(jnp.concatenate([hi, lo], axis=1), c["ones2"], 1, 0)
        return [sums[p * rows:(p + 1) * rows] for p in P]

    def cumsum(xs):
        out = []
        for p in range(0, len(xs), 2):
            hi, lo = _split(jnp.concatenate(xs[p:p + 2], axis=1))
            both = _dg(c["tril2"], jnp.concatenate([hi, lo], axis=0), 1, 0)
            out += [both[:, q * PAIR:(q + 1) * PAIR] for q in range(len(xs[p:p + 2]))]
        return out

    def softplus(z):
        return jnp.maximum(z, 0.0) + jnp.log1p(jnp.exp(-jnp.abs(z)))

    logd = [-jnp.exp(-softplus(-(w0[p] + wls[p])) - 0.5) for p in P]
    lr = [jax.nn.sigmoid(a0[p] + als[p]) for p in P]
    kkr = [ks[p] * kkp[p] for p in P]
    k2 = [ks[p] * (1.0 + (lr[p] - 1.0) * kap[p]) for p in P]
    hs = headsum([jnp.concatenate([kkr[p] * kkr[p], rs[p] * k2[p] * rkp[p]], axis=0) for p in P])
    cum = cumsum(logd)
    kk = [kkr[p] / jnp.maximum(jnp.sqrt(hs[p][:L]), 1e-12) for p in P]
    bonus = [hs[p][L:] for p in P]
    p_t = [jnp.exp(cum[p]) for p in P]
    inv_p = [jnp.exp(-cum[p]) for p in P]
    rt = [rs[p] * p_t[p] for p in P]
    at = [-kk[p] * jnp.exp(cum[p] - logd[p]) for p in P]
    bt = [kk[p] * lr[p] * inv_p[p] for p in P]
    kt = [k2[p] * inv_p[p] for p in P]

    bks = [jnp.concatenate([stack(bt[p]), stack(kt[p])], axis=0) for p in P]
    am = [_mm(jnp.concatenate([at[p], rt[p]], axis=0), bks[p], 1, 1) for p in P]
    n = [jnp.where(c["strict"], am[p][:L, :PAIR], 0.0) for p in P]
    aak = [jnp.where(c["strict"], am[p][:L, PAIR:], 0.0) for p in P]
    mrbk = [jnp.concatenate([jnp.where(c["incl"], am[p][L:, :PAIR], 0.0),
                             jnp.where(c["incl"], am[p][L:, PAIR:], 0.0)], axis=1) for p in P]
    akv = [_mm(aak[p], stack(vs[p]), 1, 0) for p in P]

    npow = [jnp.where(c["diag"], n[p], 0.0) for p in P]
    tinv = [c["eye"] + npow[p] for p in P]
    span = 2
    while span < INV_BASE:
        npow = [_mm(npow[p], stack(npow[p]), 1, 0) for p in P]
        tinv = [tinv[p] + _mm(tinv[p], stack(npow[p]), 1, 0) for p in P]
        span *= 2
    for sub in c["sub"]:
        low = [_mm(jnp.where(sub, n[p], 0.0), stack(tinv[p]), 1, 0) for p in P]
        tinv = [tinv[p] + _mm(tinv[p], stack(low[p]), 1, 0) for p in P]

    tu = [_mm(tinv[p], jnp.concatenate([stack(at[p]), stack(akv[p])], axis=1), 1, 0) for p in P]
    us = [_mm(jnp.concatenate([tu[p][:, :PAIR], rt[p]], axis=0), s0s[p], 1, 1) for p in P]
    u = [us[p][:L] + tu[p][:, PAIR:] for p in P]
    p_last = [p_t[p][L - 1:L, :] for p in P]
    s_new = [jnp.where(c["blockdiag"],
                       s0s[p] * p_last[p] + _mm(jnp.concatenate([u[p], vs[p]], axis=0),
                                                 jnp.concatenate([bt[p], kt[p]], axis=0) * p_last[p], 0, 0),
                       0.0) for p in P]
    y = [us[p][L:] + _mm(mrbk[p], jnp.concatenate([stack(u[p]), stack(vs[p])], axis=0), 1, 0) for p in P]

    mean = headsum(y)
    dlt = [y[p] - mean[p] * (1.0 / HEAD) for p in P]
    var = headsum([dlt[p] * dlt[p] for p in P])
    outs = [((dlt[p] * lax.rsqrt(var[p] * (1.0 / HEAD) + LNX_EPS) * lng[p] + lnb[p]) + bonus[p] * vs[p]) * gs[p]
            for p in P]
    return outs, s_new


def _wkv_consts(L):
    lane = lax.broadcasted_iota(jnp.int32, (L, PAIR), 1)
    row = lax.broadcasted_iota(jnp.int32, (L, PAIR), 0)
    src = lane % HEAD
    r2 = lax.broadcasted_iota(jnp.int32, (PAIR, PAIR), 0) // HEAD
    c2 = lax.broadcasted_iota(jnp.int32, (PAIR, PAIR), 1) // HEAD
    blockdiag = r2 == c2
    ones = jnp.where(blockdiag, 1.0, 0.0).astype(BF16)
    tr = lax.broadcasted_iota(jnp.int32, (L, L), 0)
    tc = lax.broadcasted_iota(jnp.int32, (L, L), 1)
    tril = jnp.where(tr >= tc, 1.0, 0.0).astype(BF16)
    sub = []
    q = INV_BASE
    while q < L:
        sub.append((row // (2 * q) == src // (2 * q)) & (row // q == src // q + 1))
        q *= 2
    return {
        "head0": lane < HEAD,
        "strict": row > src,
        "incl": row >= src,
        "eye": jnp.where(row == src, 1.0, 0.0),
        "diag": row // INV_BASE == src // INV_BASE,
        "sub": sub,
        "blockdiag": blockdiag,
        "ones2": jnp.concatenate([ones, ones], axis=0),
        "tril2": jnp.concatenate([tril, tril], axis=1),
    }


def _wkv_kernel(r_ref, k_ref, v_ref, wl_ref, al_ref, g_ref, prm_ref, o_ref, s_ref, *, pairs):
    @pl.when(pl.program_id(2) == 0)
    def _():
        s_ref[...] = jnp.zeros_like(s_ref)

    c = _wkv_consts(r_ref.shape[0])
    cols = [slice(p * PAIR, (p + 1) * PAIR) for p in range(pairs)]
    outs, s_new = _wkv_chunk(*[[ref[:, cs] for cs in cols]
                               for ref in (r_ref, k_ref, v_ref, wl_ref, al_ref, g_ref, prm_ref)],
                             [s_ref[p] for p in range(pairs)], c)
    for p in range(pairs):
        s_ref[p] = s_new[p]
        o_ref[:, cols[p]] = outs[p].astype(o_ref.dtype)


def _wkv(r, k, v, wl, al, g, prm, batch, seq_len):
    m, d = r.shape
    L = _tile(seq_len, CHUNK)
    lanes = _tile(d, LANES_WKV)
    nchunk = seq_len // L
    tok = pl.BlockSpec((L, lanes), lambda b, hg, ci: (b * nchunk + ci, hg))
    return pl.pallas_call(
        functools.partial(_wkv_kernel, pairs=lanes // PAIR),
        grid=(batch, d // lanes, nchunk),
        in_specs=[tok] * 6 + [pl.BlockSpec((SUBLANES, lanes), lambda b, hg, ci: (0, hg))],
        out_specs=tok,
        out_shape=jax.ShapeDtypeStruct((m, d), BF16),
        scratch_shapes=[pltpu.VMEM((lanes // PAIR, PAIR, PAIR), F32)],
        compiler_params=_params("arbitrary", "arbitrary", "arbitrary"),
        name="wkv7_chunk",
    )(r, k, v, wl, al, g, prm)


def _pad_lora(w1, w2):
    r = w1.shape[1]
    rp = -(-r // 128) * 128
    return (jnp.pad(w1, ((0, 0), (0, rp - r))).astype(BF16),
            jnp.pad(w2, ((0, rp - r), (0, 0))).astype(BF16))


def _cast_pad_kernel(x_ref, o_ref, *, nblocks):
    keep = pl.program_id(0) < nblocks
    o_ref[...] = jnp.where(keep, x_ref[...], 0.0).astype(o_ref.dtype)


def _cast_pad(w, axis, start, size, out_size):
    depth = w.shape[0]
    tb = next(t for t in (512, 256, 128) if start % t == 0 and size % t == 0 and out_size % t == 0)
    nblocks, first = size // tb, start // tb
    block = (depth, tb, w.shape[2]) if axis == 1 else (depth, w.shape[1], tb)

    def src(j):
        jj = first + jnp.minimum(j, nblocks - 1)
        return (0, jj, 0) if axis == 1 else (0, 0, jj)

    def dst(j):
        return (0, j, 0) if axis == 1 else (0, 0, j)

    out_shape = (depth, out_size, w.shape[2]) if axis == 1 else (depth, w.shape[1], out_size)
    return pl.pallas_call(
        functools.partial(_cast_pad_kernel, nblocks=nblocks),
        grid=(out_size // tb,),
        in_specs=[pl.BlockSpec(block, src)],
        out_specs=pl.BlockSpec(block, dst),
        out_shape=jax.ShapeDtypeStruct(out_shape, BF16),
        compiler_params=_params("arbitrary"),
        name="cast_pad",
    )(w)


def kernel(x, c, ada_w, ada_b, norm_g, pool_w, pool_scale, rwkv_mu, rwkv_wr, rwkv_wk, rwkv_wv,
           rwkv_w0, rwkv_w1, rwkv_w2, rwkv_a0, rwkv_a1, rwkv_a2, rwkv_g1, rwkv_g2, rwkv_kk,
           rwkv_ka, rwkv_rk, rwkv_lnx_g, rwkv_lnx_b, rwkv_wo, ffn_w_up, ffn_conv_w, ffn_conv_b,
           ffn_w_down, final_g):
    b, t, d = x.shape
    m = b * t
    depth = ada_w.shape[0]
    assert depth == 2, "layer 0 is the pooling mixer, layer 1 the RWKV-7 mixer"

    mod = _ada(c, ada_w, ada_b).reshape(depth, b, N_ADA, 1, d)
    sh1, sc1, gt1, sh2, sc2, gt2 = [[mod[l, :, n] for l in range(depth)] for n in range(N_ADA)]
    ng = norm_g.reshape(depth, 2, 1, d)

    f = ffn_w_down.shape[1]
    fpad = -f % min(f, COLS_FFN)
    w_gate = _cast_pad(ffn_w_up, 2, 0, f, f + fpad)
    w_val = _cast_pad(ffn_w_up, 2, f, f, f + fpad)
    w_down = _cast_pad(ffn_w_down, 1, 0, f, f + fpad)
    conv_w = jnp.pad(ffn_conv_w, ((0, 0), (0, 0), (0, fpad)))
    conv_b = jnp.pad(ffn_conv_b, ((0, 0), (0, fpad))).reshape(depth, 1, f + fpad)

    def ffn(h, l):
        return _ffn(h.reshape(m, d), w_gate, w_val, conv_w, conv_b, w_down, l, t).reshape(b, t, d)

    x1, h = _pool_layer(x, ng[0, 0], sc1[0], sh1[0], gt1[0], pool_w[0].astype(BF16),
                        pool_scale[0].reshape(1, d), ng[0, 1], sc2[0], sh2[0])
    y = ffn(h, 0)

    mu = rwkv_mu[0][jnp.array([0, 2, 3, 1, 4, 5])]
    x2, mix = _resnorm_mix(x1, y, gt2[0], ng[1, 0], sc1[1], sh1[1], mu)
    mix = mix.reshape(N_SHIFT_MIX, m, d)
    r, k, v = [_matmul(mix, n, w[0].astype(BF16)) for n, w in enumerate((rwkv_wr, rwkv_wk, rwkv_wv))]
    wl = _lora(mix, 3, *_pad_lora(rwkv_w1[0], rwkv_w2[0]), "tanh")
    al = _lora(mix, 4, *_pad_lora(rwkv_a1[0], rwkv_a2[0]), "none")
    gg = _lora(mix, 5, *_pad_lora(rwkv_g1[0], rwkv_g2[0]), "sigmoid")
    prm = jnp.stack([rwkv_w0[0], rwkv_a0[0], rwkv_kk[0], rwkv_ka[0], rwkv_rk[0].reshape(d),
                     rwkv_lnx_g[0], rwkv_lnx_b[0], jnp.zeros((d,), F32)])
    z = _wkv(r, k, v, wl, al, gg, prm, b, t)
    y = _matmul(z[None], 0, rwkv_wo[0].astype(BF16)).reshape(b, t, d)
    x3, h = _resnorm(x2, y, gt1[1], ng[1, 1], sc2[1], sh2[1])
    y = ffn(h, 1)
    return _final(x3, y, gt2[1], final_g.reshape(1, d))
```
